```python
import math
import jax, jax.numpy as jnp
from jax import lax
import numpy as np

D_MODEL = 2048
BATCH = 8
SEQ = 2048
DEPTH = 2
DEC_BATCH = 128
DEC_SEQ = 1
PAST_LEN = 2048
PAGE_SIZE = 128

MOBA_HEADS = D_MODEL // 256
MOBA_HEAD_DIM = 128
MOBA_WIDTH = MOBA_HEADS * MOBA_HEAD_DIM
MOBA_BLOCK = 256
MOBA_TOPK = 3
MOBA_QCHUNK = 64
DIFF_HEADS = D_MODEL // 256
DIFF_HEAD_DIM = 64
DIFF_V_DIM = 2 * DIFF_HEAD_DIM
DIFF_QK_WIDTH = DIFF_HEADS * 2 * DIFF_HEAD_DIM
DIFF_V_WIDTH = DIFF_HEADS * DIFF_V_DIM
D_FF = 4 * D_MODEL
IN_WIDTH = 3 * MOBA_WIDTH + 2 * DIFF_QK_WIDTH + DIFF_V_WIDTH + 2 * D_MODEL
ATTN_QBLOCK = 128
ROPE_THETA = 10000.0
NORM_EPS = 1e-6
NEG_INF = -1e30

kernel_name = "hybrid_moba_diffattn_decoder_step"


def rms_norm(x, g):
    xf = x.astype(jnp.float32)
    y = xf * lax.rsqrt(jnp.mean(xf * xf, axis=-1, keepdims=True) + NORM_EPS)
    return (y * g.astype(jnp.float32)).astype(x.dtype)


def rope(x, pos):
    half = x.shape[-1] // 2
    inv = jnp.exp(-math.log(ROPE_THETA) * jnp.arange(half, dtype=jnp.float32) / half)
    ang = pos.astype(jnp.float32)[:, None] * inv[None, :]
    cos = jnp.cos(ang)[:, None, :]
    sin = jnp.sin(ang)[:, None, :]
    xf = x.astype(jnp.float32)
    x1, x2 = xf[..., :half], xf[..., half:]
    return jnp.concatenate([x1 * cos - x2 * sin, x2 * cos + x1 * sin], axis=-1).astype(x.dtype)


def _moba_chunk(kv_b, kmean_b, q_c, p_c):
    nb, blk, _, n_h, hd = kv_b.shape
    qc = q_c.shape[0]
    n_sel = min(MOBA_TOPK, nb)
    scale = hd ** -0.5
    own = p_c // blk
    bscore = jnp.einsum('qhd,nhd->qhn', q_c.astype(jnp.float32), kmean_b)
    fully_past = jnp.arange(nb)[None, None, :] < own[:, None, None]
    bscore = jnp.where(fully_past, bscore, NEG_INF)
    _, sel = lax.top_k(bscore, n_sel)
    sel_ok = sel < own[:, None, None]
    h_idx = jnp.arange(n_h)[None, :, None]
    kv_sel = kv_b[sel, :, :, h_idx]
    kv_own = kv_b[own]
    s_sel = jnp.einsum('qhd,qhskd->qhsk', q_c, kv_sel[..., 0, :], preferred_element_type=jnp.float32) * scale
    s_sel = jnp.where(sel_ok[..., None], s_sel, NEG_INF).reshape(qc, n_h, n_sel * blk)
    s_own = jnp.einsum('qhd,qkhd->qhk', q_c, kv_own[:, :, 0], preferred_element_type=jnp.float32) * scale
    own_pos = own[:, None] * blk + jnp.arange(blk)[None, :]
    s_own = jnp.where((own_pos <= p_c[:, None])[:, None, :], s_own, NEG_INF)
    p = jax.nn.softmax(jnp.concatenate([s_sel, s_own], axis=-1), axis=-1).astype(q_c.dtype)
    p_sel = p[..., :n_sel * blk].reshape(qc, n_h, n_sel, blk)
    p_own = p[..., n_sel * blk:]
    o = (jnp.einsum('qhsk,qhskd->qhd', p_sel, kv_sel[..., 1, :], preferred_element_type=jnp.float32)
         + jnp.einsum('qhk,qkhd->qhd', p_own, kv_own[:, :, 1], preferred_element_type=jnp.float32))
    return o.astype(q_c.dtype)


def moba_attention(q, kv, q_pos):
    n_b, n_q, n_h, hd = q.shape
    t = kv.shape[1]
    nb = -(-t // MOBA_BLOCK)
    kv = jnp.pad(kv, ((0, 0), (0, nb * MOBA_BLOCK - t), (0, 0), (0, 0), (0, 0)))
    kvb = kv.reshape(n_b, nb, MOBA_BLOCK, 2, n_h, hd)
    kmean = jnp.mean(kvb, axis=2, dtype=jnp.float32)[:, :, 0]
    qc = min(MOBA_QCHUNK, n_q)
    nqc = -(-n_q // qc)
    qp = jnp.pad(q, ((0, 0), (0, nqc * qc - n_q), (0, 0), (0, 0))).reshape(n_b, nqc, qc, n_h, hd)
    pp = jnp.pad(q_pos, (0, nqc * qc - n_q)).reshape(nqc, qc)

    def per_seq(args):
        kv_b, km_b, q_b = args
        return lax.map(lambda a: _moba_chunk(kv_b, km_b, a[0], a[1]), (q_b, pp))

    o = lax.map(per_seq, (kvb, kmean, qp))
    return o.reshape(n_b, nqc * qc, n_h, hd)[:, :n_q]


def diff_attention(q1, q2, kv, q_pos, lam):
    n_b, n_q, n_h, d = q1.shape
    t = kv.shape[1]
    k1 = kv[:, :, 0, :, :d]
    k2 = kv[:, :, 0, :, d:]
    v = kv[:, :, 1]
    scale = d ** -0.5
    qc = min(ATTN_QBLOCK, n_q)
    nqc = -(-n_q // qc)
    padq = ((0, 0), (0, nqc * qc - n_q), (0, 0), (0, 0))
    q1b = jnp.moveaxis(jnp.pad(q1, padq).reshape(n_b, nqc, qc, n_h, d), 1, 0)
    q2b = jnp.moveaxis(jnp.pad(q2, padq).reshape(n_b, nqc, qc, n_h, d), 1, 0)
    pp = jnp.pad(q_pos, (0, nqc * qc - n_q)).reshape(nqc, qc)
    key_pos = jnp.arange(t)

    def per_block(args):
        q1c, q2c, pc = args
        causal = (key_pos[None, :] <= pc[:, None])[None, None]
        s1 = jnp.einsum('bqhd,bkhd->bhqk', q1c, k1, preferred_element_type=jnp.float32) * scale
        s2 = jnp.einsum('bqhd,bkhd->bhqk', q2c, k2, preferred_element_type=jnp.float32) * scale
        a1 = jax.nn.softmax(jnp.where(causal, s1, NEG_INF), axis=-1)
        a2 = jax.nn.softmax(jnp.where(causal, s2, NEG_INF), axis=-1)
        a = (a1 - lam * a2).astype(v.dtype)
        return jnp.einsum('bhqk,bkhd->bqhd', a, v, preferred_element_type=jnp.float32).astype(q1c.dtype)

    o = lax.map(per_block, (q1b, q2b, pp))
    return jnp.moveaxis(o, 0, 1).reshape(n_b, nqc * qc, n_h, v.shape[-1])[:, :n_q]


def decoder_layer(x, pos, past_moba, past_diff, lambda_init, w_in, w_mix_a, w_mix_b, w_out,
                  lq1, lk1, lq2, lk2, g_diff, g_mix_pre, g_mix_post, g_mlp_pre, g_mlp_post, w_up, w_down):
    n_b, n_l, _ = x.shape
    h = rms_norm(x, g_mix_pre)
    proj = h @ w_in
    sizes = [MOBA_WIDTH, MOBA_WIDTH, MOBA_WIDTH, DIFF_QK_WIDTH, DIFF_QK_WIDTH, DIFF_V_WIDTH, D_MODEL, D_MODEL]
    offs = np.cumsum(sizes)[:-1].tolist()
    qa, ka, va, qb, kb, vb, ga, gb = jnp.split(proj, offs, axis=-1)
    qa = rope(qa.reshape(n_b, n_l, MOBA_HEADS, MOBA_HEAD_DIM), pos)
    ka = rope(ka.reshape(n_b, n_l, MOBA_HEADS, MOBA_HEAD_DIM), pos)
    va = va.reshape(n_b, n_l, MOBA_HEADS, MOBA_HEAD_DIM)
    new_moba = jnp.stack([ka, va], axis=2)
    kv_a = new_moba if past_moba is None else jnp.concatenate([past_moba, new_moba], axis=1)
    o_a = moba_attention(qa, kv_a, pos).reshape(n_b, n_l, MOBA_WIDTH)
    qb = qb.reshape(n_b, n_l, DIFF_HEADS, 2, DIFF_HEAD_DIM)
    kb = kb.reshape(n_b, n_l, DIFF_HEADS, 2, DIFF_HEAD_DIM)
    q1, q2 = rope(qb[:, :, :, 0], pos), rope(qb[:, :, :, 1], pos)
    k1, k2 = rope(kb[:, :, :, 0], pos), rope(kb[:, :, :, 1], pos)
    vb = vb.reshape(n_b, n_l, DIFF_HEADS, DIFF_V_DIM)
    new_diff = jnp.stack([jnp.concatenate([k1, k2], axis=-1), vb], axis=2)
    kv_b = new_diff if past_diff is None else jnp.concatenate([past_diff, new_diff], axis=1)
    f32 = jnp.float32
    lam = (jnp.exp(jnp.sum(lq1.astype(f32) * lk1.astype(f32)))
           - jnp.exp(jnp.sum(lq2.astype(f32) * lk2.astype(f32))) + lambda_init)
    o_b = diff_attention(q1, q2, kv_b, pos, lam)
    o_b = (rms_norm(o_b, g_diff) * (1.0 - lambda_init)).reshape(n_b, n_l, DIFF_V_WIDTH)
    merged = jax.nn.sigmoid(ga) * (o_a @ w_mix_a) + jax.nn.sigmoid(gb) * (o_b @ w_mix_b)
    x = x + rms_norm(merged @ w_out, g_mix_post)
    u = rms_norm(x, g_mlp_pre) @ w_up
    f = jnp.square(jax.nn.relu(u)) @ w_down
    x = x + rms_norm(f, g_mlp_post)
    return x, new_moba, new_diff


def setup_inputs(seed: int = 0) -> dict:
    key = jax.random.key(seed)
    ks = jax.random.split(key, 24)
    n_pages = PAST_LEN // PAGE_SIZE
    n_pool = (5 * DEC_BATCH * n_pages) // 4
    nrm = jax.random.normal

    def dense(k, shape):
        return nrm(k, shape, jnp.float32) * (shape[-2] ** -0.5)

    def gain(k, shape):
        return 1.0 + 0.02 * nrm(k, shape, jnp.float32)

    page_table = jax.random.permutation(ks[4], n_pool)[:DEC_BATCH * n_pages].reshape(DEC_BATCH, n_pages).astype(jnp.int32)
    return {
        "x_prompt": nrm(ks[0], (BATCH, SEQ, D_MODEL), jnp.float32),
        "x_sample": nrm(ks[1], (DEC_BATCH, DEC_SEQ, D_MODEL), jnp.float32),
        "cache_moba_kv": nrm(ks[2], (DEPTH, n_pool, PAGE_SIZE, 2, MOBA_HEADS, MOBA_HEAD_DIM), jnp.float32),
        "cache_diff_kv": nrm(ks[3], (DEPTH, n_pool, PAGE_SIZE, 2, DIFF_HEADS, DIFF_V_DIM), jnp.float32),
        "page_table": page_table,
        "w_in": dense(ks[5], (DEPTH, D_MODEL, IN_WIDTH)),
        "w_mix_a": dense(ks[6], (DEPTH, MOBA_WIDTH, D_MODEL)),
        "w_mix_b": dense(ks[7], (DEPTH, DIFF_V_WIDTH, D_MODEL)),
        "w_out": dense(ks[8], (DEPTH, D_MODEL, D_MODEL)),
        "lam_q1": 0.1 * nrm(ks[9], (DEPTH, DIFF_HEAD_DIM), jnp.float32),
        "lam_k1": 0.1 * nrm(ks[10], (DEPTH, DIFF_HEAD_DIM), jnp.float32),
        "lam_q2": 0.1 * nrm(ks[11], (DEPTH, DIFF_HEAD_DIM), jnp.float32),
        "lam_k2": 0.1 * nrm(ks[12], (DEPTH, DIFF_HEAD_DIM), jnp.float32),
        "g_diff": gain(ks[13], (DEPTH, DIFF_V_DIM)),
        "g_mix_pre": gain(ks[14], (DEPTH, D_MODEL)),
        "g_mix_post": gain(ks[15], (DEPTH, D_MODEL)),
        "g_mlp_pre": gain(ks[16], (DEPTH, D_MODEL)),
        "g_mlp_post": gain(ks[17], (DEPTH, D_MODEL)),
        "w_up": dense(ks[18], (DEPTH, D_MODEL, D_FF)),
        "w_down": dense(ks[19], (DEPTH, D_FF, D_MODEL)),
    }


def reference(x_prompt, x_sample, cache_moba_kv, cache_diff_kv, page_table, w_in, w_mix_a, w_mix_b, w_out,
              lam_q1, lam_k1, lam_q2, lam_k2, g_diff, g_mix_pre, g_mix_post, g_mlp_pre, g_mlp_post, w_up, w_down):
    n_seq, n_pages = page_table.shape
    past_len = n_pages * cache_moba_kv.shape[2]
    pos_p = jnp.arange(x_prompt.shape[1], dtype=jnp.int32)
    pos_s = past_len + jnp.arange(x_sample.shape[1], dtype=jnp.int32)
    xp, xs = x_prompt, x_sample
    new_mp, new_ms, new_dp, new_ds = [], [], [], []
    for l in range(DEPTH):
        params = (w_in[l], w_mix_a[l], w_mix_b[l], w_out[l], lam_q1[l], lam_k1[l], lam_q2[l], lam_k2[l],
                  g_diff[l], g_mix_pre[l], g_mix_post[l], g_mlp_pre[l], g_mlp_post[l], w_up[l], w_down[l])
        lambda_init = 0.8 - 0.6 * math.exp(-0.3 * l)
        past_m = cache_moba_kv[l, page_table].reshape(n_seq, past_len, 2, MOBA_HEADS, MOBA_HEAD_DIM)
        past_d = cache_diff_kv[l, page_table].reshape(n_seq, past_len, 2, DIFF_HEADS, DIFF_V_DIM)
        xp, kv_m, kv_d = decoder_layer(xp, pos_p, None, None, lambda_init, *params)
        new_mp.append(kv_m)
        new_dp.append(kv_d)
        xs, kv_m, kv_d = decoder_layer(xs, pos_s, past_m, past_d, lambda_init, *params)
        new_ms.append(kv_m)
        new_ds.append(kv_d)
    return (xp, xs, jnp.stack(new_mp), jnp.stack(new_ms), jnp.stack(new_dp), jnp.stack(new_ds))
```

```python
import functools
import math

import jax
import jax.numpy as jnp
from jax import lax
from jax.experimental import pallas as pl
from jax.experimental.pallas import tpu as pltpu

F32 = jnp.float32
BF16 = jnp.bfloat16

D_MODEL = 2048
N_HEADS = 8
HEAD_W = 128
SEC_W = N_HEADS * HEAD_W
IN_WIDTH = 6 * SEC_W + 2 * D_MODEL
D_FF = 4 * D_MODEL
MOBA_BLOCK = 256
MOBA_TOPK = 3
DIFF_HALF = 64
ROPE_THETA = 10000.0
NORM_EPS = 1e-6
NEG_INF = -1e30

QA_CB, KA_CB, VA_CB, QB_CB, KB_CB, VB_CB = 0, 8, 16, 24, 32, 40
GA_COL_BLOCK, GB_COL_BLOCK = 3, 4

VMEM_LIMIT = 56 * 1024 * 1024


def _params(sem, vmem=VMEM_LIMIT):
    return pltpu.CompilerParams(dimension_semantics=sem, vmem_limit_bytes=vmem)


def _lambda_init(layer):
    return 0.8 - 0.6 * math.exp(-0.3 * layer)


IN_TM = 512
IN_TN = SEC_W


def _swap_halves(x, half):
    if 2 * half == x.shape[-1]:
        return pltpu.roll(x, half, axis=1)
    lane = lax.broadcasted_iota(jnp.int32, x.shape, 1)
    up = pltpu.roll(x, x.shape[-1] - half, axis=1)
    down = pltpu.roll(x, half, axis=1)
    return jnp.where((lane % (2 * half)) < half, up, down)


def _inproj_kernel(x_ref, g_ref, w_ref, cm_ref, sm_ref, cd_ref, sd_ref, o_ref, h_scr):
    j = pl.program_id(1)

    @pl.when(j == 0)
    def _():
        x = x_ref[...]
        ms = jnp.mean(x * x, axis=-1, keepdims=True)
        h_scr[...] = ((x * lax.rsqrt(ms + NORM_EPS)) * g_ref[...]).astype(BF16)

    o_ref[...] = jnp.dot(h_scr[...], w_ref[...], preferred_element_type=F32)

    def rope(cos_ref, sin_ref, half):
        cos = cos_ref[...]
        sin = sin_ref[...]
        for h in range(N_HEADS):
            sl = slice(h * HEAD_W, (h + 1) * HEAD_W)
            seg = o_ref[:, sl]
            o_ref[:, sl] = seg * cos + _swap_halves(seg, half) * sin

    @pl.when((j == 0) | (j == 1))
    def _():
        rope(cm_ref, sm_ref, HEAD_W // 2)

    @pl.when((j == 3) | (j == 4))
    def _():
        rope(cd_ref, sd_ref, DIFF_HALF // 2)


def _inproj(x, g, w, tabs, n_prompt_tiles, tiles_per_seq):
    t = x.shape[0]

    def tab_map(i, j):
        return (jnp.where(i < n_prompt_tiles, i % tiles_per_seq, tiles_per_seq), 0)

    tab_spec = pl.BlockSpec((IN_TM, HEAD_W), tab_map)
    return pl.pallas_call(
        _inproj_kernel,
        grid=(pl.cdiv(t, IN_TM), IN_WIDTH // IN_TN),
        in_specs=[
            pl.BlockSpec((IN_TM, D_MODEL), lambda i, j: (i, 0)),
            pl.BlockSpec((1, D_MODEL), lambda i, j: (0, 0)),
            pl.BlockSpec((D_MODEL, IN_TN), lambda i, j: (0, j)),
            tab_spec, tab_spec, tab_spec, tab_spec,
        ],
        out_specs=pl.BlockSpec((IN_TM, IN_TN), lambda i, j: (i, j)),
        out_shape=jax.ShapeDtypeStruct((t, IN_WIDTH), F32),
        scratch_shapes=[pltpu.VMEM((IN_TM, D_MODEL), BF16)],
        compiler_params=_params(("parallel", "arbitrary")),
        name="inproj",
    )(x, g, w, *tabs)


ATT_T = MOBA_BLOCK
N_KV_BLOCKS = 2048 // ATT_T


def _lam_value(lq1, lk1, lq2, lk2, lambda_init):
    a = jnp.sum(lq1[...] * lk1[...], axis=-1, keepdims=True)
    b = jnp.sum(lq2[...] * lk2[...], axis=-1, keepdims=True)
    return jnp.exp(a) - jnp.exp(b) + lambda_init


def _stage_kv(k_ref, v_ref, k_scr, vt_scr, km_scr):
    for n in range(N_KV_BLOCKS):
        rows = slice(n * ATT_T, (n + 1) * ATT_T)
        kb = k_ref[rows, :]
        k_scr[n] = kb.astype(BF16)
        vt_scr[n] = v_ref[rows, :].T.astype(BF16)
        if km_scr is not None:
            km_scr[pl.ds(n, 1), :] = jnp.mean(kb, axis=0, keepdims=True)


def _flash_step(carry, k_blk, vt_blk, qs, mask):
    m, l, acc = carry
    s = lax.dot_general(k_blk, qs, (((1,), (1,)), ((), ())), preferred_element_type=F32)
    if mask is not None:
        s = jnp.where(mask, s, NEG_INF)
    m_new = jnp.maximum(m, jnp.max(s, axis=0, keepdims=True))
    alpha = jnp.exp(m - m_new)
    p = jnp.exp(s - m_new)
    l = alpha * l + jnp.sum(p, axis=0, keepdims=True)
    acc = alpha * acc + jnp.dot(vt_blk, p.astype(BF16), preferred_element_type=F32)
    return m_new, l, acc


def _causal_mask(nq):
    key = lax.broadcasted_iota(jnp.int32, (ATT_T, nq), 0)
    qry = lax.broadcasted_iota(jnp.int32, (ATT_T, nq), 1) % ATT_T
    return key <= qry


def _moba_prompt_kernel(q_ref, k_ref, v_ref, o_ref, k_scr, vt_scr, km_scr, sel_scr):
    qt = pl.program_id(2)

    @pl.when(qt == 0)
    def _():
        _stage_kv(k_ref, v_ref, k_scr, vt_scr, km_scr)

    q = q_ref[...]
    bs = lax.dot_general(km_scr[...], q, (((1,), (1,)), ((), ())),
                         precision=lax.Precision.HIGHEST, preferred_element_type=F32)
    blk = lax.broadcasted_iota(jnp.int32, bs.shape, 0)
    bs = jnp.where(blk < qt, bs, NEG_INF)
    rank = jnp.zeros(bs.shape, jnp.int32)
    for mth in range(N_KV_BLOCKS):
        row = bs[mth:mth + 1, :]
        beats = (row > bs) | ((row == bs) & (mth < blk))
        rank = rank + jnp.where(beats, 1, 0)
    sel = jnp.where((blk < qt) & (rank < MOBA_TOPK), 1.0, 0.0)
    for n in range(N_KV_BLOCKS):
        sel_scr[n] = sel[n:n + 1, :]

    qs = (q * (HEAD_W ** -0.5)).astype(BF16)
    init = (jnp.full((1, ATT_T), NEG_INF, F32), jnp.zeros((1, ATT_T), F32),
            jnp.zeros((HEAD_W, ATT_T), F32))
    carry = _flash_step(init, k_scr[qt], vt_scr[qt], qs, _causal_mask(ATT_T))

    def body(n, c):
        return _flash_step(c, k_scr[n], vt_scr[n], qs, sel_scr[n] > 0.5)

    m, l, acc = lax.fori_loop(0, qt, body, carry)
    o_ref[...] = (acc / l).T.astype(o_ref.dtype)


def _diff_prompt_kernel(lq1, lk1, lq2, lk2, g_ref, q_ref, k_ref, v_ref, o_ref, k_scr, vt_scr,
                        *, lambda_init):
    qt = pl.program_id(2)

    @pl.when(qt == 0)
    def _():
        _stage_kv(k_ref, v_ref, k_scr, vt_scr, None)

    q = q_ref[...] * (DIFF_HALF ** -0.5)
    lane = lax.broadcasted_iota(jnp.int32, q.shape, 1)
    q1 = jnp.where(lane < DIFF_HALF, q, 0.0)
    q2 = jnp.where(lane < DIFF_HALF, 0.0, q)
    qs = jnp.concatenate([q1, q2], axis=0).astype(BF16)
    nq = 2 * ATT_T
    init = (jnp.full((1, nq), NEG_INF, F32), jnp.zeros((1, nq), F32), jnp.zeros((HEAD_W, nq), F32))
    carry = _flash_step(init, k_scr[qt], vt_scr[qt], qs, _causal_mask(nq))

    def body(n, c):
        return _flash_step(c, k_scr[n], vt_scr[n], qs, None)

    m, l, acc = lax.fori_loop(0, qt, body, carry)
    ot = acc / l
    lam = _lam_value(lq1, lk1, lq2, lk2, lambda_init)
    o = (ot[:, :ATT_T] - lam * ot[:, ATT_T:]).T
    ms = jnp.mean(o * o, axis=-1, keepdims=True)
    o = ((o * lax.rsqrt(ms + NORM_EPS)) * g_ref[...]) * (1.0 - lambda_init)
    o_ref[...] = o.astype(o_ref.dtype)


def _prompt_attention(proj, n_batch, seq, layer_params, lambda_init):
    lq1, lk1, lq2, lk2, g_diff = layer_params
    n_qt = seq // ATT_T
    grid = (n_batch, N_HEADS, n_qt)

    def q_spec(cb):
        return pl.BlockSpec((ATT_T, HEAD_W), lambda b, h, t: (b * n_qt + t, cb + h))

    def kv_spec(cb):
        return pl.BlockSpec((seq, HEAD_W), lambda b, h, t: (b, cb + h))

    out_spec = pl.BlockSpec((ATT_T, HEAD_W), lambda b, h, t: (b * n_qt + t, h))
    out_shape = jax.ShapeDtypeStruct((n_batch * seq, SEC_W), BF16)
    kv_scratch = [pltpu.VMEM((N_KV_BLOCKS, ATT_T, HEAD_W), BF16),
                  pltpu.VMEM((N_KV_BLOCKS, HEAD_W, ATT_T), BF16)]
    sem = ("parallel", "parallel", "arbitrary")

    o_a = pl.pallas_call(
        _moba_prompt_kernel,
        grid=grid,
        in_specs=[q_spec(QA_CB), kv_spec(KA_CB), kv_spec(VA_CB)],
        out_specs=out_spec,
        out_shape=out_shape,
        scratch_shapes=kv_scratch + [pltpu.VMEM((N_KV_BLOCKS, HEAD_W), F32),
                                     pltpu.VMEM((N_KV_BLOCKS, 1, ATT_T), F32)],
        compiler_params=_params(sem),
        name="moba_prompt",
    )(proj, proj, proj)

    small = pl.BlockSpec((1, DIFF_HALF), lambda b, h, t: (0, 0))
    o_b = pl.pallas_call(
        functools.partial(_diff_prompt_kernel, lambda_init=lambda_init),
        grid=grid,
        in_specs=[small, small, small, small,
                  pl.BlockSpec((1, HEAD_W), lambda b, h, t: (0, 0)),
                  q_spec(QB_CB), kv_spec(KB_CB), kv_spec(VB_CB)],
        out_specs=out_spec,
        out_shape=out_shape,
        scratch_shapes=kv_scratch,
        compiler_params=_params(sem),
        name="diff_prompt",
    )(lq1, lk1, lq2, lk2, g_diff, proj, proj, proj)
    return o_a, o_b


PAGES_PER_STEP = 4
PAGE_ROWS = 128


def _page_kv(page_ref):
    k = page_ref[:, 0]
    v = page_ref[:, 1]
    kf = k.reshape(PAGE_ROWS * N_HEADS, HEAD_W).astype(BF16)
    vf = v.reshape(PAGE_ROWS * N_HEADS, HEAD_W).astype(BF16)
    return k, kf, vf


def _head_match(n_rows):
    shape = (n_rows, PAGE_ROWS * N_HEADS)
    col = lax.broadcasted_iota(jnp.int32, shape, 1)
    row = lax.broadcasted_iota(jnp.int32, shape, 0)
    return (col % N_HEADS) == (row % N_HEADS)


def _moba_sample_kernel(pt_ref, q_ref, kn_ref, vn_ref, *rest, n_steps):
    pages = rest[:PAGES_PER_STEP]
    o_ref = rest[PAGES_PER_STEP]
    m_scr, l_scr, o_scr, ks_scr = rest[PAGES_PER_STEP + 1:]
    c = pl.program_id(1)
    scale = HEAD_W ** -0.5
    q = q_ref[...]
    qs = (q * scale).astype(BF16)
    valid = _head_match(N_HEADS)

    for i in range(PAGES_PER_STEP):
        k, kf, vf = _page_kv(pages[i])
        s = lax.dot_general(qs, kf, (((1,), (1,)), ((), ())), preferred_element_type=F32)
        s = jnp.where(valid, s, NEG_INF)
        m = jnp.max(s, axis=-1, keepdims=True)
        p = jnp.where(valid, jnp.exp(s - m), 0.0)
        pg = c * PAGES_PER_STEP + i
        m_scr[pg] = m
        l_scr[pg] = jnp.sum(p, axis=-1, keepdims=True)
        o_scr[pg] = jnp.dot(p.astype(BF16), vf, preferred_element_type=F32)
        ks_scr[pg] = jnp.sum(k, axis=0)

    @pl.when(c == n_steps - 1)
    def _():
        n_pages = n_steps * PAGES_PER_STEP
        pages_per_block = MOBA_BLOCK // PAGE_ROWS
        n_blocks = n_pages // pages_per_block
        bs = []
        for n in range(n_blocks):
            ksum = ks_scr[n * pages_per_block]
            for r in range(1, pages_per_block):
                ksum = ksum + ks_scr[n * pages_per_block + r]
            bs.append(jnp.sum(q * (ksum * (1.0 / MOBA_BLOCK)), axis=-1, keepdims=True))
        sel = []
        for n in range(n_blocks):
            rank = jnp.zeros(bs[n].shape, jnp.int32)
            for mth in range(n_blocks):
                if mth == n:
                    continue
                beats = (bs[mth] > bs[n]) | ((bs[mth] == bs[n]) & (mth < n))
                rank = rank + jnp.where(beats, 1, 0)
            sel.append(rank < MOBA_TOPK)
        kn = kn_ref[...]
        vn = vn_ref[...]
        s_self = jnp.sum((q * scale) * kn, axis=-1, keepdims=True)
        m_all = s_self
        for pg in range(n_pages):
            m_all = jnp.maximum(m_all, jnp.where(sel[pg // pages_per_block], m_scr[pg], NEG_INF))
        w_self = jnp.exp(s_self - m_all)
        l_all = w_self
        o_all = w_self * vn
        for pg in range(n_pages):
            w = jnp.where(sel[pg // pages_per_block], jnp.exp(m_scr[pg] - m_all), 0.0)
            l_all = l_all + w * l_scr[pg]
            o_all = o_all + w * o_scr[pg]
        o_ref[...] = (o_all / l_all).astype(o_ref.dtype)


def _diff_sample_kernel(pt_ref, lq1, lk1, lq2, lk2, g_ref, q_ref, kn_ref, vn_ref, *rest,
                        n_steps, lambda_init):
    pages = rest[:PAGES_PER_STEP]
    o_ref = rest[PAGES_PER_STEP]
    m_scr, l_scr, acc_scr = rest[PAGES_PER_STEP + 1:]
    c = pl.program_id(1)

    q = q_ref[...] * (DIFF_HALF ** -0.5)
    lane = lax.broadcasted_iota(jnp.int32, q.shape, 1)
    q12 = jnp.concatenate([jnp.where(lane < DIFF_HALF, q, 0.0),
                           jnp.where(lane < DIFF_HALF, 0.0, q)], axis=0)
    qs = q12.astype(BF16)
    valid = _head_match(2 * N_HEADS)

    @pl.when(c == 0)
    def _():
        m_scr[...] = jnp.full(m_scr.shape, NEG_INF, F32)
        l_scr[...] = jnp.zeros(l_scr.shape, F32)
        acc_scr[...] = jnp.zeros(acc_scr.shape, F32)

    m = m_scr[...]
    l = l_scr[...]
    acc = acc_scr[...]
    for i in range(PAGES_PER_STEP):
        _, kf, vf = _page_kv(pages[i])
        s = lax.dot_general(qs, kf, (((1,), (1,)), ((), ())), preferred_element_type=F32)
        s = jnp.where(valid, s, NEG_INF)
        m_new = jnp.maximum(m, jnp.max(s, axis=-1, keepdims=True))
        alpha = jnp.exp(m - m_new)
        p = jnp.where(valid, jnp.exp(s - m_new), 0.0)
        l = alpha * l + jnp.sum(p, axis=-1, keepdims=True)
        acc = alpha * acc + jnp.dot(p.astype(BF16), vf, preferred_element_type=F32)
        m = m_new
    m_scr[...] = m
    l_scr[...] = l
    acc_scr[...] = acc

    @pl.when(c == n_steps - 1)
    def _():
        kn = kn_ref[...]
        vn = vn_ref[...]
        s_self = jnp.sum(q12 * jnp.concatenate([kn, kn], axis=0), axis=-1, keepdims=True)
        m_f = jnp.maximum(m, s_self)
        alpha = jnp.exp(m - m_f)
        w_self = jnp.exp(s_self - m_f)
        l_f = alpha * l + w_self
        o12 = (alpha * acc + w_self * jnp.concatenate([vn, vn], axis=0)) / l_f
        lam = _lam_value(lq1, lk1, lq2, lk2, lambda_init)
        o = o12[:N_HEADS] - lam * o12[N_HEADS:]
        ms = jnp.mean(o * o, axis=-1, keepdims=True)
        o = ((o * lax.rsqrt(ms + NORM_EPS)) * g_ref[...]) * (1.0 - lambda_init)
        o_ref[...] = o.astype(o_ref.dtype)


def _sample_attention(ps3, cache_moba, cache_diff, pt_flat, layer, n_seq, n_pages,
                      layer_params, lambda_init):
    lq1, lk1, lq2, lk2, g_diff = layer_params
    n_steps = n_pages // PAGES_PER_STEP
    grid = (n_seq, n_steps)

    def sec_spec(sec):
        return pl.BlockSpec((None, N_HEADS, HEAD_W), lambda s, c, pt: (s, sec, 0))

    def page_spec(i):
        def imap(s, c, pt):
            return (layer, pt[s * n_pages + c * PAGES_PER_STEP + i], 0, 0, 0, 0)
        return pl.BlockSpec((None, None, PAGE_ROWS, 2, N_HEADS, HEAD_W), imap)

    page_specs = [page_spec(i) for i in range(PAGES_PER_STEP)]
    out_spec = pl.BlockSpec((None, N_HEADS, HEAD_W), lambda s, c, pt: (s, 0, 0))
    out_shape = jax.ShapeDtypeStruct((n_seq, N_HEADS, HEAD_W), BF16)
    sem = ("parallel", "arbitrary")

    o_a = pl.pallas_call(
        functools.partial(_moba_sample_kernel, n_steps=n_steps),
        grid_spec=pltpu.PrefetchScalarGridSpec(
            num_scalar_prefetch=1, grid=grid,
            in_specs=[sec_spec(0), sec_spec(1), sec_spec(2)] + page_specs,
            out_specs=out_spec,
            scratch_shapes=[pltpu.VMEM((n_pages, N_HEADS, 1), F32),
                            pltpu.VMEM((n_pages, N_HEADS, 1), F32),
                            pltpu.VMEM((n_pages, N_HEADS, HEAD_W), F32),
                            pltpu.VMEM((n_pages, N_HEADS, HEAD_W), F32)]),
        out_shape=out_shape,
        compiler_params=_params(sem),
        name="moba_sample",
    )(pt_flat, ps3, ps3, ps3, *([cache_moba] * PAGES_PER_STEP))

    small = pl.BlockSpec((1, DIFF_HALF), lambda s, c, pt: (0, 0))
    o_b = pl.pallas_call(
        functools.partial(_diff_sample_kernel, n_steps=n_steps, lambda_init=lambda_init),
        grid_spec=pltpu.PrefetchScalarGridSpec(
            num_scalar_prefetch=1, grid=grid,
            in_specs=[small, small, small, small,
                      pl.BlockSpec((1, HEAD_W), lambda s, c, pt: (0, 0)),
                      sec_spec(3), sec_spec(4), sec_spec(5)] + page_specs,
            out_specs=out_spec,
            scratch_shapes=[pltpu.VMEM((2 * N_HEADS, 1), F32),
                            pltpu.VMEM((2 * N_HEADS, 1), F32),
                            pltpu.VMEM((2 * N_HEADS, HEAD_W), F32)]),
        out_shape=out_shape,
        compiler_params=_params(sem),
        name="diff_sample",
    )(pt_flat, lq1, lk1, lq2, lk2, g_diff, ps3, ps3, ps3, *([cache_diff] * PAGES_PER_STEP))
    return o_a.reshape(n_seq, SEC_W), o_b.reshape(n_seq, SEC_W)


MIX_TM = 256


def _mix_kernel(oa_ref, ob_ref, ga_ref, gb_ref, x_ref, wa_ref, wb_ref, wo_ref, gpost_ref, gpre_ref,
                x1_ref, h2_ref):
    a = jnp.dot(oa_ref[...], wa_ref[...], preferred_element_type=F32)
    b = jnp.dot(ob_ref[...], wb_ref[...], preferred_element_type=F32)
    merged = jax.nn.sigmoid(ga_ref[...]) * a + jax.nn.sigmoid(gb_ref[...]) * b
    y = jnp.dot(merged.astype(BF16), wo_ref[...], preferred_element_type=F32)
    ms = jnp.mean(y * y, axis=-1, keepdims=True)
    x1 = x_ref[...] + (y * lax.rsqrt(ms + NORM_EPS)) * gpost_ref[...]
    x1_ref[...] = x1
    ms1 = jnp.mean(x1 * x1, axis=-1, keepdims=True)
    h2_ref[...] = ((x1 * lax.rsqrt(ms1 + NORM_EPS)) * gpre_ref[...]).astype(BF16)


def _mix(o_a, o_b, proj, x, wa, wb, wo, g_post, g_pre):
    t = x.shape[0]
    row = lambda w, cb=0: pl.BlockSpec((MIX_TM, w), lambda i: (i, cb))
    const = lambda shape: pl.BlockSpec(shape, lambda i: (0, 0), pipeline_mode=pl.Buffered(1))
    return pl.pallas_call(
        _mix_kernel,
        grid=(pl.cdiv(t, MIX_TM),),
        in_specs=[row(SEC_W), row(SEC_W), row(D_MODEL, GA_COL_BLOCK), row(D_MODEL, GB_COL_BLOCK),
                  row(D_MODEL),
                  const((SEC_W, D_MODEL)), const((SEC_W, D_MODEL)), const((D_MODEL, D_MODEL)),
                  const((1, D_MODEL)), const((1, D_MODEL))],
        out_specs=[row(D_MODEL), row(D_MODEL)],
        out_shape=[jax.ShapeDtypeStruct((t, D_MODEL), F32), jax.ShapeDtypeStruct((t, D_MODEL), BF16)],
        compiler_params=_params(("parallel",)),
        name="mix_out",
    )(o_a, o_b, proj, proj, x, wa, wb, wo, g_post, g_pre)


MLP_TM = 512
MLP_TF = 1024


def _mlp_kernel(h_ref, x1_ref, wu_ref, wd_ref, g_ref, o_ref, acc_scr):
    j = pl.program_id(1)
    u = jnp.dot(h_ref[...], wu_ref[...], preferred_element_type=F32)
    r = jnp.maximum(u, 0.0)
    part = jnp.dot((r * r).astype(BF16), wd_ref[...], preferred_element_type=F32)

    @pl.when(j == 0)
    def _():
        acc_scr[...] = part

    @pl.when(j > 0)
    def _():
        acc_scr[...] += part

    @pl.when(j == pl.num_programs(1) - 1)
    def _():
        f = acc_scr[...]
        ms = jnp.mean(f * f, axis=-1, keepdims=True)
        o_ref[...] = x1_ref[...] + (f * lax.rsqrt(ms + NORM_EPS)) * g_ref[...]


def _mlp(h2, x1, wu, wd, g_post):
    t = x1.shape[0]
    return pl.pallas_call(
        _mlp_kernel,
        grid=(pl.cdiv(t, MLP_TM), D_FF // MLP_TF),
        in_specs=[pl.BlockSpec((MLP_TM, D_MODEL), lambda i, j: (i, 0)),
                  pl.BlockSpec((MLP_TM, D_MODEL), lambda i, j: (i, 0)),
                  pl.BlockSpec((D_MODEL, MLP_TF), lambda i, j: (0, j)),
                  pl.BlockSpec((MLP_TF, D_MODEL), lambda i, j: (j, 0)),
                  pl.BlockSpec((1, D_MODEL), lambda i, j: (0, 0))],
        out_specs=pl.BlockSpec((MLP_TM, D_MODEL), lambda i, j: (i, 0)),
        out_shape=jax.ShapeDtypeStruct((t, D_MODEL), F32),
        scratch_shapes=[pltpu.VMEM((MLP_TM, D_MODEL), F32)],
        compiler_params=_params(("parallel", "arbitrary")),
        name="mlp",
    )(h2, x1, wu, wd, g_post)


def _rope_tables(seq, sample_pos):
    pos = jnp.concatenate([jnp.arange(seq, dtype=jnp.int32),
                           jnp.full((IN_TM,), sample_pos, jnp.int32)]).astype(F32)

    def table(half):
        inv = jnp.exp(-math.log(ROPE_THETA) * jnp.arange(half, dtype=F32) / half)
        ang = pos[:, None] * inv[None, :]
        cos, sin = jnp.cos(ang), jnp.sin(ang)
        reps = HEAD_W // (2 * half)
        return (jnp.tile(jnp.concatenate([cos, cos], axis=-1), (1, reps)),
                jnp.tile(jnp.concatenate([-sin, sin], axis=-1), (1, reps)))

    cm, sm = table(HEAD_W // 2)
    cd, sd = table(DIFF_HALF // 2)
    return cm, sm, cd, sd


def kernel(x_prompt, x_sample, cache_moba_kv, cache_diff_kv, page_table, w_in, w_mix_a, w_mix_b, w_out,
           lam_q1, lam_k1, lam_q2, lam_k2, g_diff, g_mix_pre, g_mix_post, g_mlp_pre, g_mlp_post,
           w_up, w_down):
    n_batch, seq, d_model = x_prompt.shape
    n_seq, dec_seq, _ = x_sample.shape
    depth = w_in.shape[0]
    n_pages = page_table.shape[1]
    past_len = n_pages * cache_moba_kv.shape[2]
    assert d_model == D_MODEL and dec_seq == 1 and seq % IN_TM == 0 and seq % MOBA_BLOCK == 0
    assert cache_moba_kv.shape[2] == PAGE_ROWS and past_len % MOBA_BLOCK == 0
    assert n_pages % PAGES_PER_STEP == 0 and seq == N_KV_BLOCKS * ATT_T
    n_prompt = n_batch * seq

    x = jnp.concatenate([x_prompt.reshape(n_prompt, D_MODEL), x_sample.reshape(n_seq, D_MODEL)], axis=0)
    tabs = _rope_tables(seq, past_len)
    pt_flat = page_table.reshape(-1).astype(jnp.int32)

    new_moba, new_diff = [], []
    for l in range(depth):
        lam_init = _lambda_init(l)
        row = lambda a: a[l][None, :]
        attn_params = (row(lam_q1), row(lam_k1), row(lam_q2), row(lam_k2), row(g_diff))

        proj = _inproj(x, row(g_mix_pre), w_in[l].astype(BF16), tabs, n_prompt // IN_TM, seq // IN_TM)
        new_moba.append(proj[:, KA_CB * HEAD_W:(VA_CB + N_HEADS) * HEAD_W])
        new_diff.append(proj[:, KB_CB * HEAD_W:(VB_CB + N_HEADS) * HEAD_W])

        oa_p, ob_p = _prompt_attention(proj, n_batch, seq, attn_params, lam_init)
        ps3 = proj[n_prompt:].reshape(n_seq, IN_WIDTH // HEAD_W, HEAD_W)
        oa_s, ob_s = _sample_attention(ps3, cache_moba_kv, cache_diff_kv, pt_flat, l, n_seq, n_pages,
                                       attn_params, lam_init)
        o_a = jnp.concatenate([oa_p, oa_s], axis=0)
        o_b = jnp.concatenate([ob_p, ob_s], axis=0)

        x1, h2 = _mix(o_a, o_b, proj, x, w_mix_a[l].astype(BF16), w_mix_b[l].astype(BF16),
                      w_out[l].astype(BF16), row(g_mix_post), row(g_mlp_pre))
        x = _mlp(h2, x1, w_up[l].astype(BF16), w_down[l].astype(BF16), row(g_mlp_post))

    def split(parts):
        stacked = jnp.stack(parts)
        return (stacked[:, :n_prompt].reshape(depth, n_batch, seq, 2, N_HEADS, HEAD_W),
                stacked[:, n_prompt:].reshape(depth, n_seq, 1, 2, N_HEADS, HEAD_W))

    moba_p, moba_s = split(new_moba)
    diff_p, diff_s = split(new_diff)
    return (x[:n_prompt].reshape(n_batch, seq, D_MODEL), x[n_prompt:].reshape(n_seq, 1, D_MODEL),
            moba_p, moba_s, diff_p, diff_s)
```

```python
import functools
import math

import jax
import jax.numpy as jnp
from jax import lax
from jax.experimental import pallas as pl
from jax.experimental.pallas import tpu as pltpu

F32 = jnp.float32
BF16 = jnp.bfloat16

D_MODEL = 2048
N_HEADS = 8
HEAD_W = 128
SEC_W = N_HEADS * HEAD_W
IN_WIDTH = 6 * SEC_W + 2 * D_MODEL
D_FF = 4 * D_MODEL
MOBA_BLOCK = 256
MOBA_TOPK = 3
DIFF_HALF = 64
ROPE_THETA = 10000.0
NORM_EPS = 1e-6
NEG_INF = -1e30

QA_CB, KA_CB, VA_CB, QB_CB, KB_CB, VB_CB = 0, 8, 16, 24, 32, 40
GA_COL_BLOCK, GB_COL_BLOCK = 3, 4

VMEM_LIMIT = 56 * 1024 * 1024


def _params(sem, vmem=VMEM_LIMIT):
    return pltpu.CompilerParams(dimension_semantics=sem, vmem_limit_bytes=vmem)


def _lambda_init(layer):
    return 0.8 - 0.6 * math.exp(-0.3 * layer)


IN_TM = 1024
IN_TN = SEC_W


def _swap_halves(x, half):
    if 2 * half == x.shape[-1]:
        return pltpu.roll(x, half, axis=1)
    lane = lax.broadcasted_iota(jnp.int32, x.shape, 1)
    up = pltpu.roll(x, x.shape[-1] - half, axis=1)
    down = pltpu.roll(x, half, axis=1)
    return jnp.where((lane % (2 * half)) < half, up, down)


def _inproj_kernel(x_ref, g_ref, w_ref, cm_ref, sm_ref, cd_ref, sd_ref, o_ref, h_scr):
    j = pl.program_id(1)

    @pl.when(j == 0)
    def _():
        x = x_ref[...]
        ms = jnp.mean(x * x, axis=-1, keepdims=True)
        h_scr[...] = ((x * lax.rsqrt(ms + NORM_EPS)) * g_ref[...]).astype(BF16)

    o_ref[...] = jnp.dot(h_scr[...], w_ref[...], preferred_element_type=F32)

    def rope(cos_ref, sin_ref, half):
        cos = cos_ref[...]
        sin = sin_ref[...]
        for h in range(N_HEADS):
            sl = slice(h * HEAD_W, (h + 1) * HEAD_W)
            seg = o_ref[:, sl]
            o_ref[:, sl] = seg * cos + _swap_halves(seg, half) * sin

    @pl.when((j == 0) | (j == 1))
    def _():
        rope(cm_ref, sm_ref, HEAD_W // 2)

    @pl.when((j == 3) | (j == 4))
    def _():
        rope(cd_ref, sd_ref, DIFF_HALF // 2)


def _inproj(x, g, w, tabs, n_prompt_tiles, tiles_per_seq):
    t = x.shape[0]

    def tab_map(i, j):
        return (jnp.where(i < n_prompt_tiles, i % tiles_per_seq, tiles_per_seq), 0)

    tab_spec = pl.BlockSpec((IN_TM, HEAD_W), tab_map)
    return pl.pallas_call(
        _inproj_kernel,
        grid=(pl.cdiv(t, IN_TM), IN_WIDTH // IN_TN),
        in_specs=[
            pl.BlockSpec((IN_TM, D_MODEL), lambda i, j: (i, 0)),
            pl.BlockSpec((1, D_MODEL), lambda i, j: (0, 0)),
            pl.BlockSpec((D_MODEL, IN_TN), lambda i, j: (0, j)),
            tab_spec, tab_spec, tab_spec, tab_spec,
        ],
        out_specs=pl.BlockSpec((IN_TM, IN_TN), lambda i, j: (i, j)),
        out_shape=jax.ShapeDtypeStruct((t, IN_WIDTH), F32),
        scratch_shapes=[pltpu.VMEM((IN_TM, D_MODEL), BF16)],
        compiler_params=_params(("parallel", "arbitrary")),
        name="inproj",
    )(x, g, w, *tabs)


ATT_T = MOBA_BLOCK
N_KV_BLOCKS = 2048 // ATT_T
MOBA_HEADS_PER_STEP = 4
DIFF_HEADS_PER_STEP = 2


def _lam_value(lq1, lk1, lq2, lk2, lambda_init):
    a = jnp.sum(lq1[...] * lk1[...], axis=-1, keepdims=True)
    b = jnp.sum(lq2[...] * lk2[...], axis=-1, keepdims=True)
    return jnp.exp(a) - jnp.exp(b) + lambda_init


def _stage_kv(k_ref, v_ref, k_scr, vt_scr, km_scr, n_heads):
    for hh in range(n_heads):
        cols = slice(hh * HEAD_W, (hh + 1) * HEAD_W)
        for n in range(N_KV_BLOCKS):
            rows = slice(n * ATT_T, (n + 1) * ATT_T)
            kb = k_ref[rows, cols]
            k_scr[hh, n] = kb.astype(BF16)
            vt_scr[hh, n] = v_ref[rows, cols].T.astype(BF16)
            if km_scr is not None:
                km_scr[hh, n:n + 1, :] = jnp.mean(kb, axis=0, keepdims=True)


def _flash_chains(carries, k_blks, vt_blks, qss, masks):
    scores = [lax.dot_general(k, q, (((1,), (1,)), ((), ())), preferred_element_type=F32)
              for k, q in zip(k_blks, qss)]
    soft = []
    for (m, l, acc), s, mask in zip(carries, scores, masks):
        if mask is not None:
            s = jnp.where(mask, s, NEG_INF)
        m_new = jnp.maximum(m, jnp.max(s, axis=0, keepdims=True))
        alpha = jnp.exp(m - m_new)
        p = jnp.exp(s - m_new)
        l = alpha * l + jnp.sum(p, axis=0, keepdims=True)
        soft.append((m_new, l, alpha * acc, p.astype(BF16)))
    return tuple((m_new, l, acc + jnp.dot(vt, p, preferred_element_type=F32))
                 for (m_new, l, acc, p), vt in zip(soft, vt_blks))


def _chain_init(n_chains):
    return tuple((jnp.full((1, ATT_T), NEG_INF, F32), jnp.zeros((1, ATT_T), F32),
                  jnp.zeros((HEAD_W, ATT_T), F32)) for _ in range(n_chains))


def _causal_mask():
    key = lax.broadcasted_iota(jnp.int32, (ATT_T, ATT_T), 0)
    qry = lax.broadcasted_iota(jnp.int32, (ATT_T, ATT_T), 1)
    return key <= qry


def _moba_select(km, q, qt):
    bs = lax.dot_general(km, q, (((1,), (1,)), ((), ())),
                         precision=lax.Precision.HIGHEST, preferred_element_type=F32)
    blk = lax.broadcasted_iota(jnp.int32, bs.shape, 0)
    bs = jnp.where(blk < qt, bs, NEG_INF)
    rank = jnp.zeros(bs.shape, jnp.int32)
    for mth in range(N_KV_BLOCKS):
        row = bs[mth:mth + 1, :]
        beats = (row > bs) | ((row == bs) & (mth < blk))
        rank = rank + jnp.where(beats, 1, 0)
    return jnp.where((blk < qt) & (rank < MOBA_TOPK), 1.0, 0.0)


def _moba_prompt_kernel(q_ref, k_ref, v_ref, o_ref, k_scr, vt_scr, km_scr, sel_scr):
    nh = MOBA_HEADS_PER_STEP
    qt = pl.program_id(2)

    @pl.when(qt == 0)
    def _():
        _stage_kv(k_ref, v_ref, k_scr, vt_scr, km_scr, nh)

    qss = []
    for hh in range(nh):
        q = q_ref[:, hh * HEAD_W:(hh + 1) * HEAD_W]
        sel = _moba_select(km_scr[hh], q, qt)
        for n in range(N_KV_BLOCKS):
            sel_scr[hh, n] = sel[n:n + 1, :]
        qss.append((q * (HEAD_W ** -0.5)).astype(BF16))

    causal = _causal_mask()
    carry = _flash_chains(_chain_init(nh), [k_scr[hh, qt] for hh in range(nh)],
                          [vt_scr[hh, qt] for hh in range(nh)], qss, [causal] * nh)

    def body(n, c):
        return _flash_chains(c, [k_scr[hh, n] for hh in range(nh)],
                             [vt_scr[hh, n] for hh in range(nh)], qss,
                             [sel_scr[hh, n] > 0.5 for hh in range(nh)])

    carry = lax.fori_loop(0, qt, body, carry)
    for hh in range(nh):
        m, l, acc = carry[hh]
        o_ref[:, hh * HEAD_W:(hh + 1) * HEAD_W] = (acc / l).T.astype(o_ref.dtype)


def _diff_prompt_kernel(lq1, lk1, lq2, lk2, g_ref, q_ref, k_ref, v_ref, o_ref, k_scr, vt_scr,
                        *, lambda_init):
    nh = DIFF_HEADS_PER_STEP
    qt = pl.program_id(2)

    @pl.when(qt == 0)
    def _():
        _stage_kv(k_ref, v_ref, k_scr, vt_scr, None, nh)

    qss = []
    for hh in range(nh):
        q = q_ref[:, hh * HEAD_W:(hh + 1) * HEAD_W] * (DIFF_HALF ** -0.5)
        lane = lax.broadcasted_iota(jnp.int32, q.shape, 1)
        qss.append(jnp.where(lane < DIFF_HALF, q, 0.0).astype(BF16))
        qss.append(jnp.where(lane < DIFF_HALF, 0.0, q).astype(BF16))
    heads = [hh for hh in range(nh) for _ in range(2)]

    causal = _causal_mask()
    carry = _flash_chains(_chain_init(2 * nh), [k_scr[hh, qt] for hh in heads],
                          [vt_scr[hh, qt] for hh in heads], qss, [causal] * (2 * nh))

    def body(n, c):
        return _flash_chains(c, [k_scr[hh, n] for hh in heads], [vt_scr[hh, n] for hh in heads],
                             qss, [None] * (2 * nh))

    carry = lax.fori_loop(0, qt, body, carry)
    lam = _lam_value(lq1, lk1, lq2, lk2, lambda_init)
    for hh in range(nh):
        _, l1, acc1 = carry[2 * hh]
        _, l2, acc2 = carry[2 * hh + 1]
        o = (acc1 / l1 - lam * (acc2 / l2)).T
        ms = jnp.mean(o * o, axis=-1, keepdims=True)
        o = ((o * lax.rsqrt(ms + NORM_EPS)) * g_ref[...]) * (1.0 - lambda_init)
        o_ref[:, hh * HEAD_W:(hh + 1) * HEAD_W] = o.astype(o_ref.dtype)


def _prompt_attention(proj, n_batch, seq, layer_params, lambda_init):
    lq1, lk1, lq2, lk2, g_diff = layer_params
    n_qt = seq // ATT_T
    out_shape = jax.ShapeDtypeStruct((n_batch * seq, SEC_W), BF16)
    sem = ("parallel", "parallel", "arbitrary")

    def specs(nh, q_cb, k_cb, v_cb):
        w = nh * HEAD_W
        q_spec = lambda cb: pl.BlockSpec((ATT_T, w), lambda b, g, t: (b * n_qt + t, cb // nh + g))
        kv_spec = lambda cb: pl.BlockSpec((seq, w), lambda b, g, t: (b, cb // nh + g))
        scratch = [pltpu.VMEM((nh, N_KV_BLOCKS, ATT_T, HEAD_W), BF16),
                   pltpu.VMEM((nh, N_KV_BLOCKS, HEAD_W, ATT_T), BF16)]
        return (n_batch, N_HEADS // nh, n_qt), [q_spec(q_cb), kv_spec(k_cb), kv_spec(v_cb)], q_spec(0), scratch

    nh = MOBA_HEADS_PER_STEP
    grid, in_specs, out_spec, scratch = specs(nh, QA_CB, KA_CB, VA_CB)
    o_a = pl.pallas_call(
        _moba_prompt_kernel,
        grid=grid,
        in_specs=in_specs,
        out_specs=out_spec,
        out_shape=out_shape,
        scratch_shapes=scratch + [pltpu.VMEM((nh, N_KV_BLOCKS, HEAD_W), F32),
                                  pltpu.VMEM((nh, N_KV_BLOCKS, 1, ATT_T), F32)],
        compiler_params=_params(sem),
        name="moba_prompt",
    )(proj, proj, proj)

    grid, in_specs, out_spec, scratch = specs(DIFF_HEADS_PER_STEP, QB_CB, KB_CB, VB_CB)
    small = pl.BlockSpec((1, DIFF_HALF), lambda b, g, t: (0, 0))
    o_b = pl.pallas_call(
        functools.partial(_diff_prompt_kernel, lambda_init=lambda_init),
        grid=grid,
        in_specs=[small, small, small, small,
                  pl.BlockSpec((1, HEAD_W), lambda b, g, t: (0, 0))] + in_specs,
        out_specs=out_spec,
        out_shape=out_shape,
        scratch_shapes=scratch,
        compiler_params=_params(sem),
        name="diff_prompt",
    )(lq1, lk1, lq2, lk2, g_diff, proj, proj, proj)
    return o_a, o_b


PAGES_PER_STEP = 8
PAGE_ROWS = 128
PAGES_PER_BLOCK = MOBA_BLOCK // PAGE_ROWS


def _page_kv(page_ref):
    k = page_ref[:, 0]
    v = page_ref[:, 1]
    kf = k.reshape(PAGE_ROWS * N_HEADS, HEAD_W).astype(BF16)
    vf = v.reshape(PAGE_ROWS * N_HEADS, HEAD_W).astype(BF16)
    return k, kf, vf


def _head_match(n_rows):
    shape = (n_rows, PAGE_ROWS * N_HEADS)
    col = lax.broadcasted_iota(jnp.int32, shape, 1)
    row = lax.broadcasted_iota(jnp.int32, shape, 0)
    return (col % N_HEADS) == (row % N_HEADS)


def _page_scores(qs, kfs, valid):
    return [jnp.where(valid, lax.dot_general(qs, kf, (((1,), (1,)), ((), ())),
                                             preferred_element_type=F32), NEG_INF)
            for kf in kfs]


def _row_max(xs):
    m = jnp.max(xs[0], axis=-1, keepdims=True)
    for x in xs[1:]:
        m = jnp.maximum(m, jnp.max(x, axis=-1, keepdims=True))
    return m


def _moba_sample_kernel(pt_ref, q_ref, kn_ref, vn_ref, *rest, n_steps):
    pages = rest[:PAGES_PER_STEP]
    o_ref = rest[PAGES_PER_STEP]
    m_scr, l_scr, o_scr, ks_scr = rest[PAGES_PER_STEP + 1:]
    c = pl.program_id(1)
    blocks_per_step = PAGES_PER_STEP // PAGES_PER_BLOCK
    scale = HEAD_W ** -0.5
    q = q_ref[...]
    qs = (q * scale).astype(BF16)
    valid = _head_match(N_HEADS)

    ks, kfs, vfs = zip(*[_page_kv(p) for p in pages])
    ksums = [jnp.sum(k, axis=0) for k in ks]
    scores = _page_scores(qs, kfs, valid)
    soft = []
    for b in range(blocks_per_step):
        sl = slice(b * PAGES_PER_BLOCK, (b + 1) * PAGES_PER_BLOCK)
        m = _row_max(scores[sl])
        ps = [jnp.exp(s - m) for s in scores[sl]]
        l = sum(jnp.sum(p, axis=-1, keepdims=True) for p in ps)
        soft.append((m, l, [p.astype(BF16) for p in ps]))
    for b, (m, l, ps) in enumerate(soft):
        sl = slice(b * PAGES_PER_BLOCK, (b + 1) * PAGES_PER_BLOCK)
        n = c * blocks_per_step + b
        m_scr[n] = m
        l_scr[n] = l
        o_scr[n] = sum(jnp.dot(p, vf, preferred_element_type=F32) for p, vf in zip(ps, vfs[sl]))
        ks_scr[n] = sum(ksums[sl])

    @pl.when(c == n_steps - 1)
    def _():
        n_blocks = n_steps * blocks_per_step
        bs = [jnp.sum(q * (ks_scr[n] * (1.0 / MOBA_BLOCK)), axis=-1, keepdims=True)
              for n in range(n_blocks)]
        sel = []
        for n in range(n_blocks):
            rank = jnp.zeros(bs[n].shape, jnp.int32)
            for mth in range(n_blocks):
                if mth == n:
                    continue
                beats = (bs[mth] > bs[n]) | ((bs[mth] == bs[n]) & (mth < n))
                rank = rank + jnp.where(beats, 1, 0)
            sel.append(rank < MOBA_TOPK)
        kn = kn_ref[...]
        vn = vn_ref[...]
        s_self = jnp.sum((q * scale) * kn, axis=-1, keepdims=True)
        m_all = s_self
        for n in range(n_blocks):
            m_all = jnp.maximum(m_all, jnp.where(sel[n], m_scr[n], NEG_INF))
        w_self = jnp.exp(s_self - m_all)
        l_all = w_self
        o_all = w_self * vn
        for n in range(n_blocks):
            w = jnp.where(sel[n], jnp.exp(m_scr[n] - m_all), 0.0)
            l_all = l_all + w * l_scr[n]
            o_all = o_all + w * o_scr[n]
        o_ref[...] = (o_all / l_all).astype(o_ref.dtype)


def _diff_sample_kernel(pt_ref, lq1, lk1, lq2, lk2, g_ref, q_ref, kn_ref, vn_ref, *rest,
                        n_steps, lambda_init):
    pages = rest[:PAGES_PER_STEP]
    o_ref = rest[PAGES_PER_STEP]
    m_scr, l_scr, acc_scr = rest[PAGES_PER_STEP + 1:]
    c = pl.program_id(1)

    q = q_ref[...] * (DIFF_HALF ** -0.5)
    lane = lax.broadcasted_iota(jnp.int32, q.shape, 1)
    q12 = jnp.concatenate([jnp.where(lane < DIFF_HALF, q, 0.0),
                           jnp.where(lane < DIFF_HALF, 0.0, q)], axis=0)
    qs = q12.astype(BF16)
    valid = _head_match(2 * N_HEADS)

    @pl.when(c == 0)
    def _():
        m_scr[...] = jnp.full(m_scr.shape, NEG_INF, F32)
        l_scr[...] = jnp.zeros(l_scr.shape, F32)
        acc_scr[...] = jnp.zeros(acc_scr.shape, F32)

    _, kfs, vfs = zip(*[_page_kv(p) for p in pages])
    scores = _page_scores(qs, kfs, valid)
    m = m_scr[...]
    m_new = jnp.maximum(m, _row_max(scores))
    alpha = jnp.exp(m - m_new)
    ps = [jnp.exp(s - m_new) for s in scores]
    l = alpha * l_scr[...] + sum(jnp.sum(p, axis=-1, keepdims=True) for p in ps)
    acc = alpha * acc_scr[...] + sum(jnp.dot(p.astype(BF16), vf, preferred_element_type=F32)
                                     for p, vf in zip(ps, vfs))
    m_scr[...] = m_new
    l_scr[...] = l
    acc_scr[...] = acc

    @pl.when(c == n_steps - 1)
    def _():
        kn = kn_ref[...]
        vn = vn_ref[...]
        s_self = jnp.sum(q12 * jnp.concatenate([kn, kn], axis=0), axis=-1, keepdims=True)
        m_f = jnp.maximum(m_new, s_self)
        a_f = jnp.exp(m_new - m_f)
        w_self = jnp.exp(s_self - m_f)
        l_f = a_f * l + w_self
        o12 = (a_f * acc + w_self * jnp.concatenate([vn, vn], axis=0)) / l_f
        lam = _lam_value(lq1, lk1, lq2, lk2, lambda_init)
        o = o12[:N_HEADS] - lam * o12[N_HEADS:]
        ms = jnp.mean(o * o, axis=-1, keepdims=True)
        o = ((o * lax.rsqrt(ms + NORM_EPS)) * g_ref[...]) * (1.0 - lambda_init)
        o_ref[...] = o.astype(o_ref.dtype)


def _sample_attention(ps3, cache_moba, cache_diff, pt_flat, layer, n_seq, n_pages,
                      layer_params, lambda_init):
    lq1, lk1, lq2, lk2, g_diff = layer_params
    n_steps = n_pages // PAGES_PER_STEP
    n_blocks = n_pages // PAGES_PER_BLOCK
    grid = (n_seq, n_steps)

    def sec_spec(sec):
        return pl.BlockSpec((None, N_HEADS, HEAD_W), lambda s, c, pt: (s, sec, 0))

    def page_spec(i):
        def imap(s, c, pt):
            return (layer, pt[s * n_pages + c * PAGES_PER_STEP + i], 0, 0, 0, 0)
        return pl.BlockSpec((None, None, PAGE_ROWS, 2, N_HEADS, HEAD_W), imap)

    page_specs = [page_spec(i) for i in range(PAGES_PER_STEP)]
    out_spec = pl.BlockSpec((None, N_HEADS, HEAD_W), lambda s, c, pt: (s, 0, 0))
    out_shape = jax.ShapeDtypeStruct((n_seq, N_HEADS, HEAD_W), BF16)
    sem = ("parallel", "arbitrary")

    o_a = pl.pallas_call(
        functools.partial(_moba_sample_kernel, n_steps=n_steps),
        grid_spec=pltpu.PrefetchScalarGridSpec(
            num_scalar_prefetch=1, grid=grid,
            in_specs=[sec_spec(0), sec_spec(1), sec_spec(2)] + page_specs,
            out_specs=out_spec,
            scratch_shapes=[pltpu.VMEM((n_blocks, N_HEADS, 1), F32),
                            pltpu.VMEM((n_blocks, N_HEADS, 1), F32),
                            pltpu.VMEM((n_blocks, N_HEADS, HEAD_W), F32),
                            pltpu.VMEM((n_blocks, N_HEADS, HEAD_W), F32)]),
        out_shape=out_shape,
        compiler_params=_params(sem),
        name="moba_sample",
    )(pt_flat, ps3, ps3, ps3, *([cache_moba] * PAGES_PER_STEP))

    small = pl.BlockSpec((1, DIFF_HALF), lambda s, c, pt: (0, 0))
    o_b = pl.pallas_call(
        functools.partial(_diff_sample_kernel, n_steps=n_steps, lambda_init=lambda_init),
        grid_spec=pltpu.PrefetchScalarGridSpec(
            num_scalar_prefetch=1, grid=grid,
            in_specs=[small, small, small, small,
                      pl.BlockSpec((1, HEAD_W), lambda s, c, pt: (0, 0)),
                      sec_spec(3), sec_spec(4), sec_spec(5)] + page_specs,
            out_specs=out_spec,
            scratch_shapes=[pltpu.VMEM((2 * N_HEADS, 1), F32),
                            pltpu.VMEM((2 * N_HEADS, 1), F32),
                            pltpu.VMEM((2 * N_HEADS, HEAD_W), F32)]),
        out_shape=out_shape,
        compiler_params=_params(sem),
        name="diff_sample",
    )(pt_flat, lq1, lk1, lq2, lk2, g_diff, ps3, ps3, ps3, *([cache_diff] * PAGES_PER_STEP))
    return o_a.reshape(n_seq, SEC_W), o_b.reshape(n_seq, SEC_W)


MIX_TM = 256


def _mix_kernel(oa_ref, ob_ref, ga_ref, gb_ref, x_ref, wa_ref, wb_ref, wo_ref, gpost_ref, gpre_ref,
                x1_ref, h2_ref):
    a = jnp.dot(oa_ref[...], wa_ref[...], preferred_element_type=F32)
    b = jnp.dot(ob_ref[...], wb_ref[...], preferred_element_type=F32)
    merged = jax.nn.sigmoid(ga_ref[...]) * a + jax.nn.sigmoid(gb_ref[...]) * b
    y = jnp.dot(merged.astype(BF16), wo_ref[...], preferred_element_type=F32)
    ms = jnp.mean(y * y, axis=-1, keepdims=True)
    x1 = x_ref[...] + (y * lax.rsqrt(ms + NORM_EPS)) * gpost_ref[...]
    x1_ref[...] = x1
    ms1 = jnp.mean(x1 * x1, axis=-1, keepdims=True)
    h2_ref[...] = ((x1 * lax.rsqrt(ms1 + NORM_EPS)) * gpre_ref[...]).astype(BF16)


def _mix(o_a, o_b, proj, x, wa, wb, wo, g_post, g_pre):
    t = x.shape[0]
    row = lambda w, cb=0: pl.BlockSpec((MIX_TM, w), lambda i: (i, cb))
    const = lambda shape: pl.BlockSpec(shape, lambda i: (0, 0), pipeline_mode=pl.Buffered(1))
    return pl.pallas_call(
        _mix_kernel,
        grid=(pl.cdiv(t, MIX_TM),),
        in_specs=[row(SEC_W), row(SEC_W), row(D_MODEL, GA_COL_BLOCK), row(D_MODEL, GB_COL_BLOCK),
                  row(D_MODEL),
                  const((SEC_W, D_MODEL)), const((SEC_W, D_MODEL)), const((D_MODEL, D_MODEL)),
                  const((1, D_MODEL)), const((1, D_MODEL))],
        out_specs=[row(D_MODEL), row(D_MODEL)],
        out_shape=[jax.ShapeDtypeStruct((t, D_MODEL), F32), jax.ShapeDtypeStruct((t, D_MODEL), BF16)],
        compiler_params=_params(("parallel",)),
        name="mix_out",
    )(o_a, o_b, proj, proj, x, wa, wb, wo, g_post, g_pre)


MLP_TM = 512
MLP_TF = 1024


def _mlp_kernel(h_ref, x1_ref, wu_ref, wd_ref, g_ref, o_ref, acc_scr):
    j = pl.program_id(1)
    u = jnp.dot(h_ref[...], wu_ref[...], preferred_element_type=F32)
    r = jnp.maximum(u, 0.0)
    part = jnp.dot((r * r).astype(BF16), wd_ref[...], preferred_element_type=F32)

    @pl.when(j == 0)
    def _():
        acc_scr[...] = part

    @pl.when(j > 0)
    def _():
        acc_scr[...] += part

    @pl.when(j == pl.num_programs(1) - 1)
    def _():
        f = acc_scr[...]
        ms = jnp.mean(f * f, axis=-1, keepdims=True)
        o_ref[...] = x1_ref[...] + (f * lax.rsqrt(ms + NORM_EPS)) * g_ref[...]


def _mlp(h2, x1, wu, wd, g_post):
    t = x1.shape[0]
    return pl.pallas_call(
        _mlp_kernel,
        grid=(pl.cdiv(t, MLP_TM), D_FF // MLP_TF),
        in_specs=[pl.BlockSpec((MLP_TM, D_MODEL), lambda i, j: (i, 0)),
                  pl.BlockSpec((MLP_TM, D_MODEL), lambda i, j: (i, 0)),
                  pl.BlockSpec((D_MODEL, MLP_TF), lambda i, j: (0, j)),
                  pl.BlockSpec((MLP_TF, D_MODEL), lambda i, j: (j, 0)),
                  pl.BlockSpec((1, D_MODEL), lambda i, j: (0, 0))],
        out_specs=pl.BlockSpec((MLP_TM, D_MODEL), lambda i, j: (i, 0)),
        out_shape=jax.ShapeDtypeStruct((t, D_MODEL), F32),
        scratch_shapes=[pltpu.VMEM((MLP_TM, D_MODEL), F32)],
        compiler_params=_params(("parallel", "arbitrary")),
        name="mlp",
    )(h2, x1, wu, wd, g_post)


def _rope_tables(seq, sample_pos):
    pos = jnp.concatenate([jnp.arange(seq, dtype=jnp.int32),
                           jnp.full((IN_TM,), sample_pos, jnp.int32)]).astype(F32)

    def table(half):
        inv = jnp.exp(-math.log(ROPE_THETA) * jnp.arange(half, dtype=F32) / half)
        ang = pos[:, None] * inv[None, :]
        cos, sin = jnp.cos(ang), jnp.sin(ang)
        reps = HEAD_W // (2 * half)
        return (jnp.tile(jnp.concatenate([cos, cos], axis=-1), (1, reps)),
                jnp.tile(jnp.concatenate([-sin, sin], axis=-1), (1, reps)))

    cm, sm = table(HEAD_W // 2)
    cd, sd = table(DIFF_HALF // 2)
    return cm, sm, cd, sd


def kernel(x_prompt, x_sample, cache_moba_kv, cache_diff_kv, page_table, w_in, w_mix_a, w_mix_b, w_out,
           lam_q1, lam_k1, lam_q2, lam_k2, g_diff, g_mix_pre, g_mix_post, g_mlp_pre, g_mlp_post,
           w_up, w_down):
    n_batch, seq, d_model = x_prompt.shape
    n_seq, dec_seq, _ = x_sample.shape
    depth = w_in.shape[0]
    n_pages = page_table.shape[1]
    past_len = n_pages * cache_moba_kv.shape[2]
    assert d_model == D_MODEL and dec_seq == 1 and seq % IN_TM == 0 and seq % MOBA_BLOCK == 0
    assert cache_moba_kv.shape[2] == PAGE_ROWS and past_len % MOBA_BLOCK == 0
    assert n_pages % PAGES_PER_STEP == 0 and seq == N_KV_BLOCKS * ATT_T
    n_prompt = n_batch * seq

    x = jnp.concatenate([x_prompt.reshape(n_prompt, D_MODEL), x_sample.reshape(n_seq, D_MODEL)], axis=0)
    tabs = _rope_tables(seq, past_len)
    pt_flat = page_table.reshape(-1).astype(jnp.int32)

    new_moba, new_diff = [], []
    for l in range(depth):
        lam_init = _lambda_init(l)
        row = lambda a: a[l][None, :]
        attn_params = (row(lam_q1), row(lam_k1), row(lam_q2), row(lam_k2), row(g_diff))

        proj = _inproj(x, row(g_mix_pre), w_in[l].astype(BF16), tabs, n_prompt // IN_TM, seq // IN_TM)
        new_moba.append(proj[:, KA_CB * HEAD_W:(VA_CB + N_HEADS) * HEAD_W])
        new_diff.append(proj[:, KB_CB * HEAD_W:(VB_CB + N_HEADS) * HEAD_W])

        oa_p, ob_p = _prompt_attention(proj, n_batch, seq, attn_params, lam_init)
        ps3 = proj[n_prompt:].reshape(n_seq, IN_WIDTH // HEAD_W, HEAD_W)
        oa_s, ob_s = _sample_attention(ps3, cache_moba_kv, cache_diff_kv, pt_flat, l, n_seq, n_pages,
                                       attn_params, lam_init)
        o_a = jnp.concatenate([oa_p, oa_s], axis=0)
        o_b = jnp.concatenate([ob_p, ob_s], axis=0)

        x1, h2 = _mix(o_a, o_b, proj, x, w_mix_a[l].astype(BF16), w_mix_b[l].astype(BF16),
                      w_out[l].astype(BF16), row(g_mix_post), row(g_mlp_pre))
        x = _mlp(h2, x1, w_up[l].astype(BF16), w_down[l].astype(BF16), row(g_mlp_post))

    def split(parts):
        stacked = jnp.stack(parts)
        return (stacked[:, :n_prompt].reshape(depth, n_batch, seq, 2, N_HEADS, HEAD_W),
                stacked[:, n_prompt:].reshape(depth, n_seq, 1, 2, N_HEADS, HEAD_W))

    moba_p, moba_s = split(new_moba)
    diff_p, diff_s = split(new_diff)
    return (x[:n_prompt].reshape(n_batch, seq, D_MODEL), x[n_prompt:].reshape(n_seq, 1, D_MODEL),
            moba_p, moba_s, diff_p, diff_s)
```

```python
import functools
import math

import jax
import jax.numpy as jnp
from jax import lax
from jax.experimental import pallas as pl
from jax.experimental.pallas import tpu as pltpu

F32 = jnp.float32
BF16 = jnp.bfloat16

D_MODEL = 2048
N_HEADS = 8
HEAD_W = 128
SEC_W = N_HEADS * HEAD_W
N_SECTIONS = 10
IN_WIDTH = N_SECTIONS * SEC_W
D_FF = 4 * D_MODEL
MOBA_BLOCK = 256
MOBA_TOPK = 3
DIFF_HALF = 64
ROPE_THETA = 10000.0
NORM_EPS = 1e-6
NEG_INF = -1e30
LOG2E = 1.4426950408889634

VMEM_LIMIT = 56 * 1024 * 1024


def _params(sem, vmem=VMEM_LIMIT):
    return pltpu.CompilerParams(dimension_semantics=sem, vmem_limit_bytes=vmem)


def _lambda_init(layer):
    return 0.8 - 0.6 * math.exp(-0.3 * layer)


IN_TM = 1024
IN_TN = SEC_W
QA_SEC, KA_SEC, VA_SEC, QB_SEC, KB_SEC, VB_SEC = range(6)
GATE_A_BLOCK = 3


def _swap_halves(x, half):
    if 2 * half == x.shape[-1]:
        return pltpu.roll(x, half, axis=1)
    lane = lax.broadcasted_iota(jnp.int32, x.shape, 1)
    up = pltpu.roll(x, x.shape[-1] - half, axis=1)
    down = pltpu.roll(x, half, axis=1)
    return jnp.where((lane % (2 * half)) < half, up, down)


def _inproj_kernel(x_ref, g_ref, w_ref, cm_ref, sm_ref, cd_ref, sd_ref, o_ref, h_scr):
    j = pl.program_id(1)

    @pl.when(j == 0)
    def _():
        x = x_ref[...]
        ms = jnp.mean(x * x, axis=-1, keepdims=True)
        h_scr[...] = ((x * lax.rsqrt(ms + NORM_EPS)) * g_ref[...]).astype(BF16)

    o_ref[...] = jnp.dot(h_scr[...], w_ref[...], preferred_element_type=F32)

    def rope(cos_ref, sin_ref, half):
        cos = cos_ref[...]
        sin = sin_ref[...]
        for h in range(N_HEADS):
            sl = slice(h * HEAD_W, (h + 1) * HEAD_W)
            seg = o_ref[:, sl]
            o_ref[:, sl] = seg * cos + _swap_halves(seg, half) * sin

    @pl.when((j == QA_SEC) | (j == KA_SEC))
    def _():
        rope(cm_ref, sm_ref, HEAD_W // 2)

    @pl.when((j == QB_SEC) | (j == KB_SEC))
    def _():
        rope(cd_ref, sd_ref, DIFF_HALF // 2)


def _inproj(x, g, w, tabs, tab_map):
    t = x.shape[0]
    tm = min(IN_TM, t)
    tab_spec = pl.BlockSpec((tm, HEAD_W), tab_map)
    return pl.pallas_call(
        _inproj_kernel,
        grid=(t // tm, N_SECTIONS),
        in_specs=[
            pl.BlockSpec((tm, D_MODEL), lambda i, j: (i, 0)),
            pl.BlockSpec((1, D_MODEL), lambda i, j: (0, 0)),
            pl.BlockSpec((D_MODEL, IN_TN), lambda i, j: (0, j)),
            tab_spec, tab_spec, tab_spec, tab_spec,
        ],
        out_specs=pl.BlockSpec((tm, IN_TN), lambda i, j: (i, j)),
        out_shape=jax.ShapeDtypeStruct((t, IN_WIDTH), F32),
        scratch_shapes=[pltpu.VMEM((tm, D_MODEL), BF16)],
        compiler_params=_params(("parallel", "arbitrary")),
        name="inproj",
    )(x, g, w, *tabs)


ATT_T = MOBA_BLOCK
N_KV_BLOCKS = 2048 // ATT_T
MOBA_HEADS_PER_STEP = 4
DIFF_HEADS_PER_STEP = 2


def _lam_value(lq1, lk1, lq2, lk2, lambda_init):
    a = jnp.sum(lq1[...] * lk1[...], axis=-1, keepdims=True)
    b = jnp.sum(lq2[...] * lk2[...], axis=-1, keepdims=True)
    return jnp.exp(a) - jnp.exp(b) + lambda_init


def _stage_kv(k_ref, v_ref, k_scr, vt_scr, km_scr, n_heads):
    for hh in range(n_heads):
        cols = slice(hh * HEAD_W, (hh + 1) * HEAD_W)
        for n in range(N_KV_BLOCKS):
            rows = slice(n * ATT_T, (n + 1) * ATT_T)
            kb = k_ref[rows, cols]
            k_scr[hh, n] = kb.astype(BF16)
            vt_scr[hh, n] = v_ref[rows, cols].T.astype(BF16)
            if km_scr is not None:
                km_scr[hh, n:n + 1, :] = jnp.mean(kb, axis=0, keepdims=True)


def _flash_chains(carries, k_blks, vt_blks, qss, masks):
    scores = [lax.dot_general(k, q, (((1,), (1,)), ((), ())), preferred_element_type=F32)
              for k, q in zip(k_blks, qss)]
    soft = []
    for (m, l, acc), s, mask in zip(carries, scores, masks):
        if mask is not None:
            s = jnp.where(mask, s, NEG_INF)
        m_new = jnp.maximum(m, jnp.max(s, axis=0, keepdims=True))
        alpha = jnp.exp2(m - m_new)
        p = jnp.exp2(s - m_new)
        l = alpha * l + jnp.sum(p, axis=0, keepdims=True)
        soft.append((m_new, l, alpha * acc, p.astype(BF16)))
    return tuple((m_new, l, acc + jnp.dot(vt, p, preferred_element_type=F32))
                 for (m_new, l, acc, p), vt in zip(soft, vt_blks))


def _chain_init(n_chains):
    return tuple((jnp.full((1, ATT_T), NEG_INF, F32), jnp.zeros((1, ATT_T), F32),
                  jnp.zeros((HEAD_W, ATT_T), F32)) for _ in range(n_chains))


def _causal_mask():
    key = lax.broadcasted_iota(jnp.int32, (ATT_T, ATT_T), 0)
    qry = lax.broadcasted_iota(jnp.int32, (ATT_T, ATT_T), 1)
    return key <= qry


def _moba_select(km, q, qt):
    bs = lax.dot_general(km, q, (((1,), (1,)), ((), ())),
                         precision=lax.Precision.HIGHEST, preferred_element_type=F32)
    blk = lax.broadcasted_iota(jnp.int32, bs.shape, 0)
    bs = jnp.where(blk < qt, bs, NEG_INF)
    rank = jnp.zeros(bs.shape, jnp.int32)
    for mth in range(N_KV_BLOCKS):
        row = bs[mth:mth + 1, :]
        beats = (row > bs) | ((row == bs) & (mth < blk))
        rank = rank + jnp.where(beats, 1, 0)
    return jnp.where((blk < qt) & (rank < MOBA_TOPK), 1.0, 0.0)


def _moba_prompt_kernel(q_ref, k_ref, v_ref, o_ref, k_scr, vt_scr, km_scr, sel_scr):
    nh = MOBA_HEADS_PER_STEP
    qt = pl.program_id(2)

    @pl.when(qt == 0)
    def _():
        _stage_kv(k_ref, v_ref, k_scr, vt_scr, km_scr, nh)

    qss = []
    for hh in range(nh):
        q = q_ref[:, hh * HEAD_W:(hh + 1) * HEAD_W]
        sel = _moba_select(km_scr[hh], q, qt)
        for n in range(N_KV_BLOCKS):
            sel_scr[hh, n] = sel[n:n + 1, :]
        qss.append((q * (LOG2E * HEAD_W ** -0.5)).astype(BF16))

    causal = _causal_mask()
    carry = _flash_chains(_chain_init(nh), [k_scr[hh, qt] for hh in range(nh)],
                          [vt_scr[hh, qt] for hh in range(nh)], qss, [causal] * nh)

    def body(n, c):
        return _flash_chains(c, [k_scr[hh, n] for hh in range(nh)],
                             [vt_scr[hh, n] for hh in range(nh)], qss,
                             [sel_scr[hh, n] > 0.5 for hh in range(nh)])

    carry = lax.fori_loop(0, qt, body, carry)
    for hh in range(nh):
        m, l, acc = carry[hh]
        o_ref[:, hh * HEAD_W:(hh + 1) * HEAD_W] = (acc / l).T.astype(o_ref.dtype)


def _diff_prompt_kernel(lq1, lk1, lq2, lk2, g_ref, q_ref, k_ref, v_ref, o_ref, k_scr, vt_scr,
                        *, lambda_init):
    nh = DIFF_HEADS_PER_STEP
    qt = pl.program_id(2)

    @pl.when(qt == 0)
    def _():
        _stage_kv(k_ref, v_ref, k_scr, vt_scr, None, nh)

    qss = []
    for hh in range(nh):
        q = q_ref[:, hh * HEAD_W:(hh + 1) * HEAD_W] * (LOG2E * DIFF_HALF ** -0.5)
        lane = lax.broadcasted_iota(jnp.int32, q.shape, 1)
        qss.append(jnp.where(lane < DIFF_HALF, q, 0.0).astype(BF16))
        qss.append(jnp.where(lane < DIFF_HALF, 0.0, q).astype(BF16))
    heads = [hh for hh in range(nh) for _ in range(2)]

    causal = _causal_mask()
    carry = _flash_chains(_chain_init(2 * nh), [k_scr[hh, qt] for hh in heads],
                          [vt_scr[hh, qt] for hh in heads], qss, [causal] * (2 * nh))

    def body(n, c):
        return _flash_chains(c, [k_scr[hh, n] for hh in heads], [vt_scr[hh, n] for hh in heads],
                             qss, [None] * (2 * nh))

    carry = lax.fori_loop(0, qt, body, carry)
    lam = _lam_value(lq1, lk1, lq2, lk2, lambda_init)
    for hh in range(nh):
        _, l1, acc1 = carry[2 * hh]
        _, l2, acc2 = carry[2 * hh + 1]
        o = (acc1 / l1 - lam * (acc2 / l2)).T
        ms = jnp.mean(o * o, axis=-1, keepdims=True)
        o = ((o * lax.rsqrt(ms + NORM_EPS)) * g_ref[...]) * (1.0 - lambda_init)
        o_ref[:, hh * HEAD_W:(hh + 1) * HEAD_W] = o.astype(o_ref.dtype)


def _prompt_attention(proj, n_batch, seq, layer_params, lambda_init):
    lq1, lk1, lq2, lk2, g_diff = layer_params
    n_qt = seq // ATT_T
    out_shape = jax.ShapeDtypeStruct((n_batch * seq, SEC_W), BF16)
    sem = ("parallel", "parallel", "arbitrary")

    def specs(nh, q_sec, k_sec, v_sec):
        w = nh * HEAD_W
        per_sec = SEC_W // w
        q_spec = lambda sec: pl.BlockSpec((ATT_T, w), lambda b, g, t: (b * n_qt + t, sec * per_sec + g))
        kv_spec = lambda sec: pl.BlockSpec((seq, w), lambda b, g, t: (b, sec * per_sec + g))
        scratch = [pltpu.VMEM((nh, N_KV_BLOCKS, ATT_T, HEAD_W), BF16),
                   pltpu.VMEM((nh, N_KV_BLOCKS, HEAD_W, ATT_T), BF16)]
        return ((n_batch, per_sec, n_qt), [q_spec(q_sec), kv_spec(k_sec), kv_spec(v_sec)],
                q_spec(0), scratch)

    nh = MOBA_HEADS_PER_STEP
    grid, in_specs, out_spec, scratch = specs(nh, QA_SEC, KA_SEC, VA_SEC)
    o_a = pl.pallas_call(
        _moba_prompt_kernel,
        grid=grid,
        in_specs=in_specs,
        out_specs=out_spec,
        out_shape=out_shape,
        scratch_shapes=scratch + [pltpu.VMEM((nh, N_KV_BLOCKS, HEAD_W), F32),
                                  pltpu.VMEM((nh, N_KV_BLOCKS, 1, ATT_T), F32)],
        compiler_params=_params(sem),
        name="moba_prompt",
    )(proj, proj, proj)

    grid, in_specs, out_spec, scratch = specs(DIFF_HEADS_PER_STEP, QB_SEC, KB_SEC, VB_SEC)
    small = pl.BlockSpec((1, DIFF_HALF), lambda b, g, t: (0, 0))
    o_b = pl.pallas_call(
        functools.partial(_diff_prompt_kernel, lambda_init=lambda_init),
        grid=grid,
        in_specs=[small, small, small, small,
                  pl.BlockSpec((1, HEAD_W), lambda b, g, t: (0, 0))] + in_specs,
        out_specs=out_spec,
        out_shape=out_shape,
        scratch_shapes=scratch,
        compiler_params=_params(sem),
        name="diff_prompt",
    )(lq1, lk1, lq2, lk2, g_diff, proj, proj, proj)
    return o_a, o_b


PAGES_PER_STEP = 8
PAGE_ROWS = 128
PAGES_PER_BLOCK = MOBA_BLOCK // PAGE_ROWS


def _page_kv(page_ref):
    k = page_ref[:, 0]
    v = page_ref[:, 1]
    kf = k.reshape(PAGE_ROWS * N_HEADS, HEAD_W).astype(BF16)
    vf = v.reshape(PAGE_ROWS * N_HEADS, HEAD_W).astype(BF16)
    return k, kf, vf


def _head_match(n_rows):
    shape = (n_rows, PAGE_ROWS * N_HEADS)
    col = lax.broadcasted_iota(jnp.int32, shape, 1)
    row = lax.broadcasted_iota(jnp.int32, shape, 0)
    return (col % N_HEADS) == (row % N_HEADS)


def _page_scores(qs, kfs, valid):
    return [jnp.where(valid, lax.dot_general(qs, kf, (((1,), (1,)), ((), ())),
                                             preferred_element_type=F32), NEG_INF)
            for kf in kfs]


def _row_max(xs):
    m = jnp.max(xs[0], axis=-1, keepdims=True)
    for x in xs[1:]:
        m = jnp.maximum(m, jnp.max(x, axis=-1, keepdims=True))
    return m


def _moba_sample_kernel(pt_ref, q_ref, kn_ref, vn_ref, *rest, n_steps):
    pages = rest[:PAGES_PER_STEP]
    o_ref = rest[PAGES_PER_STEP]
    m_scr, l_scr, o_scr, ks_scr = rest[PAGES_PER_STEP + 1:]
    c = pl.program_id(1)
    blocks_per_step = PAGES_PER_STEP // PAGES_PER_BLOCK
    scale = HEAD_W ** -0.5
    q = q_ref[...]
    qs = (q * scale).astype(BF16)
    valid = _head_match(N_HEADS)

    ks, kfs, vfs = zip(*[_page_kv(p) for p in pages])
    ksums = [jnp.sum(k, axis=0) for k in ks]
    scores = _page_scores(qs, kfs, valid)
    soft = []
    for b in range(blocks_per_step):
        sl = slice(b * PAGES_PER_BLOCK, (b + 1) * PAGES_PER_BLOCK)
        m = _row_max(scores[sl])
        ps = [jnp.exp(s - m) for s in scores[sl]]
        l = sum(jnp.sum(p, axis=-1, keepdims=True) for p in ps)
        soft.append((m, l, [p.astype(BF16) for p in ps]))
    for b, (m, l, ps) in enumerate(soft):
        sl = slice(b * PAGES_PER_BLOCK, (b + 1) * PAGES_PER_BLOCK)
        n = c * blocks_per_step + b
        m_scr[n] = m
        l_scr[n] = l
        o_scr[n] = sum(jnp.dot(p, vf, preferred_element_type=F32) for p, vf in zip(ps, vfs[sl]))
        ks_scr[n] = sum(ksums[sl])

    @pl.when(c == n_steps - 1)
    def _():
        n_blocks = n_steps * blocks_per_step
        bs = [jnp.sum(q * (ks_scr[n] * (1.0 / MOBA_BLOCK)), axis=-1, keepdims=True)
              for n in range(n_blocks)]
        sel = []
        for n in range(n_blocks):
            rank = jnp.zeros(bs[n].shape, jnp.int32)
            for mth in range(n_blocks):
                if mth == n:
                    continue
                beats = (bs[mth] > bs[n]) | ((bs[mth] == bs[n]) & (mth < n))
                rank = rank + jnp.where(beats, 1, 0)
            sel.append(rank < MOBA_TOPK)
        kn = kn_ref[...]
        vn = vn_ref[...]
        s_self = jnp.sum((q * scale) * kn, axis=-1, keepdims=True)
        m_all = s_self
        for n in range(n_blocks):
            m_all = jnp.maximum(m_all, jnp.where(sel[n], m_scr[n], NEG_INF))
        w_self = jnp.exp(s_self - m_all)
        l_all = w_self
        o_all = w_self * vn
        for n in range(n_blocks):
            w = jnp.where(sel[n], jnp.exp(m_scr[n] - m_all), 0.0)
            l_all = l_all + w * l_scr[n]
            o_all = o_all + w * o_scr[n]
        o_ref[...] = (o_all / l_all).astype(o_ref.dtype)


def _diff_sample_kernel(pt_ref, lq1, lk1, lq2, lk2, g_ref, q_ref, kn_ref, vn_ref, *rest,
                        n_steps, lambda_init):
    pages = rest[:PAGES_PER_STEP]
    o_ref = rest[PAGES_PER_STEP]
    m_scr, l_scr, acc_scr = rest[PAGES_PER_STEP + 1:]
    c = pl.program_id(1)

    q = q_ref[...] * (DIFF_HALF ** -0.5)
    lane = lax.broadcasted_iota(jnp.int32, q.shape, 1)
    q12 = jnp.concatenate([jnp.where(lane < DIFF_HALF, q, 0.0),
                           jnp.where(lane < DIFF_HALF, 0.0, q)], axis=0)
    qs = q12.astype(BF16)
    valid = _head_match(2 * N_HEADS)

    @pl.when(c == 0)
    def _():
        m_scr[...] = jnp.full(m_scr.shape, NEG_INF, F32)
        l_scr[...] = jnp.zeros(l_scr.shape, F32)
        acc_scr[...] = jnp.zeros(acc_scr.shape, F32)

    _, kfs, vfs = zip(*[_page_kv(p) for p in pages])
    scores = _page_scores(qs, kfs, valid)
    m = m_scr[...]
    m_new = jnp.maximum(m, _row_max(scores))
    alpha = jnp.exp(m - m_new)
    ps = [jnp.exp(s - m_new) for s in scores]
    l = alpha * l_scr[...] + sum(jnp.sum(p, axis=-1, keepdims=True) for p in ps)
    acc = alpha * acc_scr[...] + sum(jnp.dot(p.astype(BF16), vf, preferred_element_type=F32)
                                     for p, vf in zip(ps, vfs))
    m_scr[...] = m_new
    l_scr[...] = l
    acc_scr[...] = acc

    @pl.when(c == n_steps - 1)
    def _():
        kn = kn_ref[...]
        vn = vn_ref[...]
        s_self = jnp.sum(q12 * jnp.concatenate([kn, kn], axis=0), axis=-1, keepdims=True)
        m_f = jnp.maximum(m_new, s_self)
        a_f = jnp.exp(m_new - m_f)
        w_self = jnp.exp(s_self - m_f)
        l_f = a_f * l + w_self
        o12 = (a_f * acc + w_self * jnp.concatenate([vn, vn], axis=0)) / l_f
        lam = _lam_value(lq1, lk1, lq2, lk2, lambda_init)
        o = o12[:N_HEADS] - lam * o12[N_HEADS:]
        ms = jnp.mean(o * o, axis=-1, keepdims=True)
        o = ((o * lax.rsqrt(ms + NORM_EPS)) * g_ref[...]) * (1.0 - lambda_init)
        o_ref[...] = o.astype(o_ref.dtype)


def _sample_attention(ps3, cache_moba, cache_diff, pt_flat, layer, n_seq, n_pages,
                      layer_params, lambda_init):
    lq1, lk1, lq2, lk2, g_diff = layer_params
    n_steps = n_pages // PAGES_PER_STEP
    n_blocks = n_pages // PAGES_PER_BLOCK
    grid = (n_seq, n_steps)

    def sec_spec(sec):
        return pl.BlockSpec((None, N_HEADS, HEAD_W), lambda s, c, pt: (s, sec, 0))

    def page_spec(i):
        def imap(s, c, pt):
            return (layer, pt[s * n_pages + c * PAGES_PER_STEP + i], 0, 0, 0, 0)
        return pl.BlockSpec((None, None, PAGE_ROWS, 2, N_HEADS, HEAD_W), imap)

    page_specs = [page_spec(i) for i in range(PAGES_PER_STEP)]
    out_spec = pl.BlockSpec((None, N_HEADS, HEAD_W), lambda s, c, pt: (s, 0, 0))
    out_shape = jax.ShapeDtypeStruct((n_seq, N_HEADS, HEAD_W), BF16)
    sem = ("parallel", "arbitrary")

    o_a = pl.pallas_call(
        functools.partial(_moba_sample_kernel, n_steps=n_steps),
        grid_spec=pltpu.PrefetchScalarGridSpec(
            num_scalar_prefetch=1, grid=grid,
            in_specs=[sec_spec(0), sec_spec(1), sec_spec(2)] + page_specs,
            out_specs=out_spec,
            scratch_shapes=[pltpu.VMEM((n_blocks, N_HEADS, 1), F32),
                            pltpu.VMEM((n_blocks, N_HEADS, 1), F32),
                            pltpu.VMEM((n_blocks, N_HEADS, HEAD_W), F32),
                            pltpu.VMEM((n_blocks, N_HEADS, HEAD_W), F32)]),
        out_shape=out_shape,
        compiler_params=_params(sem),
        name="moba_sample",
    )(pt_flat, ps3, ps3, ps3, *([cache_moba] * PAGES_PER_STEP))

    small = pl.BlockSpec((1, DIFF_HALF), lambda s, c, pt: (0, 0))
    o_b = pl.pallas_call(
        functools.partial(_diff_sample_kernel, n_steps=n_steps, lambda_init=lambda_init),
        grid_spec=pltpu.PrefetchScalarGridSpec(
            num_scalar_prefetch=1, grid=grid,
            in_specs=[small, small, small, small,
                      pl.BlockSpec((1, HEAD_W), lambda s, c, pt: (0, 0)),
                      sec_spec(3), sec_spec(4), sec_spec(5)] + page_specs,
            out_specs=out_spec,
            scratch_shapes=[pltpu.VMEM((2 * N_HEADS, 1), F32),
                            pltpu.VMEM((2 * N_HEADS, 1), F32),
                            pltpu.VMEM((2 * N_HEADS, HEAD_W), F32)]),
        out_shape=out_shape,
        compiler_params=_params(sem),
        name="diff_sample",
    )(pt_flat, lq1, lk1, lq2, lk2, g_diff, ps3, ps3, ps3, *([cache_diff] * PAGES_PER_STEP))
    return o_a.reshape(n_seq, SEC_W), o_b.reshape(n_seq, SEC_W)


MIX_TM = 256


def _mix_kernel(oa_ref, ob_ref, ga_ref, gb_ref, x_ref, wa_ref, wb_ref, wo_ref, gpost_ref, gpre_ref,
                x1_ref, h2_ref):
    a = jnp.dot(oa_ref[...], wa_ref[...], preferred_element_type=F32)
    b = jnp.dot(ob_ref[...], wb_ref[...], preferred_element_type=F32)
    merged = jax.nn.sigmoid(ga_ref[...]) * a + jax.nn.sigmoid(gb_ref[...]) * b
    y = jnp.dot(merged.astype(BF16), wo_ref[...], preferred_element_type=F32)
    ms = jnp.mean(y * y, axis=-1, keepdims=True)
    x1 = x_ref[...] + (y * lax.rsqrt(ms + NORM_EPS)) * gpost_ref[...]
    x1_ref[...] = x1
    ms1 = jnp.mean(x1 * x1, axis=-1, keepdims=True)
    h2_ref[...] = ((x1 * lax.rsqrt(ms1 + NORM_EPS)) * gpre_ref[...]).astype(BF16)


def _mix(o_a, o_b, proj, x, wa, wb, wo, g_post, g_pre):
    t = x.shape[0]
    tm = min(MIX_TM, t)
    row = lambda w, cb=0: pl.BlockSpec((tm, w), lambda i: (i, cb))
    const = lambda shape: pl.BlockSpec(shape, lambda i: (0, 0), pipeline_mode=pl.Buffered(1))
    return pl.pallas_call(
        _mix_kernel,
        grid=(t // tm,),
        in_specs=[row(SEC_W), row(SEC_W), row(D_MODEL, GATE_A_BLOCK), row(D_MODEL, GATE_A_BLOCK + 1),
                  row(D_MODEL),
                  const((SEC_W, D_MODEL)), const((SEC_W, D_MODEL)), const((D_MODEL, D_MODEL)),
                  const((1, D_MODEL)), const((1, D_MODEL))],
        out_specs=[row(D_MODEL), row(D_MODEL)],
        out_shape=[jax.ShapeDtypeStruct((t, D_MODEL), F32), jax.ShapeDtypeStruct((t, D_MODEL), BF16)],
        compiler_params=_params(("parallel",)),
        name="mix_out",
    )(o_a, o_b, proj, proj, x, wa, wb, wo, g_post, g_pre)


MLP_TM = 512
MLP_TF = 1024


def _mlp_kernel(h_ref, x1_ref, wu_ref, wd_ref, g_ref, o_ref, acc_scr):
    j = pl.program_id(1)
    u = jnp.dot(h_ref[...], wu_ref[...], preferred_element_type=F32)
    r = jnp.maximum(u, 0.0)
    part = jnp.dot((r * r).astype(BF16), wd_ref[...], preferred_element_type=F32)

    @pl.when(j == 0)
    def _():
        acc_scr[...] = part

    @pl.when(j > 0)
    def _():
        acc_scr[...] += part

    @pl.when(j == pl.num_programs(1) - 1)
    def _():
        f = acc_scr[...]
        ms = jnp.mean(f * f, axis=-1, keepdims=True)
        o_ref[...] = x1_ref[...] + (f * lax.rsqrt(ms + NORM_EPS)) * g_ref[...]


def _mlp(h2, x1, wu, wd, g_post):
    t = x1.shape[0]
    tm = min(MLP_TM, t)
    return pl.pallas_call(
        _mlp_kernel,
        grid=(t // tm, D_FF // MLP_TF),
        in_specs=[pl.BlockSpec((tm, D_MODEL), lambda i, j: (i, 0)),
                  pl.BlockSpec((tm, D_MODEL), lambda i, j: (i, 0)),
                  pl.BlockSpec((D_MODEL, MLP_TF), lambda i, j: (0, j)),
                  pl.BlockSpec((MLP_TF, D_MODEL), lambda i, j: (j, 0)),
                  pl.BlockSpec((1, D_MODEL), lambda i, j: (0, 0))],
        out_specs=pl.BlockSpec((tm, D_MODEL), lambda i, j: (i, 0)),
        out_shape=jax.ShapeDtypeStruct((t, D_MODEL), F32),
        scratch_shapes=[pltpu.VMEM((tm, D_MODEL), F32)],
        compiler_params=_params(("parallel", "arbitrary")),
        name="mlp",
    )(h2, x1, wu, wd, g_post)


def _rope_tables(pos):
    pos = pos.astype(F32)

    def table(half):
        inv = jnp.exp(-math.log(ROPE_THETA) * jnp.arange(half, dtype=F32) / half)
        ang = pos[:, None] * inv[None, :]
        cos, sin = jnp.cos(ang), jnp.sin(ang)
        reps = HEAD_W // (2 * half)
        return (jnp.tile(jnp.concatenate([cos, cos], axis=-1), (1, reps)),
                jnp.tile(jnp.concatenate([-sin, sin], axis=-1), (1, reps)))

    cm, sm = table(HEAD_W // 2)
    cd, sd = table(DIFF_HALF // 2)
    return cm, sm, cd, sd


def kernel(x_prompt, x_sample, cache_moba_kv, cache_diff_kv, page_table, w_in, w_mix_a, w_mix_b, w_out,
           lam_q1, lam_k1, lam_q2, lam_k2, g_diff, g_mix_pre, g_mix_post, g_mlp_pre, g_mlp_post,
           w_up, w_down):
    n_batch, seq, d_model = x_prompt.shape
    n_seq, dec_seq, _ = x_sample.shape
    depth = w_in.shape[0]
    n_pages = page_table.shape[1]
    past_len = n_pages * cache_moba_kv.shape[2]
    assert d_model == D_MODEL and dec_seq == 1 and seq % IN_TM == 0 and seq == N_KV_BLOCKS * ATT_T
    assert cache_moba_kv.shape[2] == PAGE_ROWS and past_len % MOBA_BLOCK == 0
    assert n_pages % PAGES_PER_STEP == 0 and n_seq % 8 == 0 and n_seq <= MIX_TM
    n_prompt = n_batch * seq
    tiles_per_seq = seq // IN_TM

    xp = x_prompt.reshape(n_prompt, D_MODEL)
    xs = x_sample.reshape(n_seq, D_MODEL)
    tabs_p = _rope_tables(jnp.arange(seq, dtype=jnp.int32))
    tabs_s = _rope_tables(jnp.full((n_seq,), past_len, jnp.int32))
    pt_flat = page_table.reshape(-1).astype(jnp.int32)

    kv_cols = lambda p, k_sec: p[:, k_sec * SEC_W:(k_sec + 2) * SEC_W]
    new_moba_p, new_diff_p, new_moba_s, new_diff_s = [], [], [], []
    for l in range(depth):
        lam_init = _lambda_init(l)
        row = lambda a: a[l][None, :]
        attn_params = (row(lam_q1), row(lam_k1), row(lam_q2), row(lam_k2), row(g_diff))
        w_in_l = w_in[l].astype(BF16)
        wa, wb, wo = w_mix_a[l].astype(BF16), w_mix_b[l].astype(BF16), w_out[l].astype(BF16)
        wu, wd = w_up[l].astype(BF16), w_down[l].astype(BF16)

        proj_p = _inproj(xp, row(g_mix_pre), w_in_l, tabs_p, lambda i, j: (i % tiles_per_seq, 0))
        new_moba_p.append(kv_cols(proj_p, KA_SEC))
        new_diff_p.append(kv_cols(proj_p, KB_SEC))
        oa_p, ob_p = _prompt_attention(proj_p, n_batch, seq, attn_params, lam_init)
        x1, h2 = _mix(oa_p, ob_p, proj_p, xp, wa, wb, wo, row(g_mix_post), row(g_mlp_pre))
        xp = _mlp(h2, x1, wu, wd, row(g_mlp_post))

        proj_s = _inproj(xs, row(g_mix_pre), w_in_l, tabs_s, lambda i, j: (0, 0))
        new_moba_s.append(kv_cols(proj_s, KA_SEC))
        new_diff_s.append(kv_cols(proj_s, KB_SEC))
        ps3 = proj_s.reshape(n_seq, IN_WIDTH // HEAD_W, HEAD_W)
        oa_s, ob_s = _sample_attention(ps3, cache_moba_kv, cache_diff_kv, pt_flat, l, n_seq, n_pages,
                                       attn_params, lam_init)
        x1, h2 = _mix(oa_s, ob_s, proj_s, xs, wa, wb, wo, row(g_mix_post), row(g_mlp_pre))
        xs = _mlp(h2, x1, wu, wd, row(g_mlp_post))

    kv_shape_p = (depth, n_batch, seq, 2, N_HEADS, HEAD_W)
    kv_shape_s = (depth, n_seq, 1, 2, N_HEADS, HEAD_W)
    return (xp.reshape(n_batch, seq, D_MODEL), xs.reshape(n_seq, 1, D_MODEL),
            jnp.stack(new_moba_p).reshape(kv_shape_p), jnp.stack(new_moba_s).reshape(kv_shape_s),
            jnp.stack(new_diff_p).reshape(kv_shape_p), jnp.stack(new_diff_s).reshape(kv_shape_s))
```
